```python
import jax, jax.numpy as jnp
from jax import lax
import numpy as np

D_MODEL = 2048
BATCH = 2
SEQ = 8192
DEPTH = 1
DEC_BATCH = 2
DEC_SEQ = 4096
PAST_LEN = 128

MIX_WIDTH = D_MODEL
CONV_WIDTH = MIX_WIDTH // 2
CONV_KERNEL = 31
CONV_PAD = CONV_KERNEL // 2
HG_WIDTH = MIX_WIDTH - CONV_WIDTH
HG_HEADS = 8
HG_DK = HG_WIDTH // HG_HEADS
HG_DV = HG_WIDTH // HG_HEADS
D_FF = 5632
CHUNK = 64
NORM_EPS = 1e-6
SPLIT_SIZES = (CONV_WIDTH, CONV_WIDTH, HG_WIDTH, HG_WIDTH, HG_WIDTH, HG_WIDTH, HG_WIDTH)
SPLIT_IDX = tuple(int(v) for v in np.cumsum(SPLIT_SIZES)[:-1])
IN_COLS = int(sum(SPLIT_SIZES))

kernel_name = "hymba_conformer_hgrn2_macaron_encoder"


def rms_norm(x, w):
    xf = x.astype(jnp.float32)
    y = xf * lax.rsqrt(jnp.mean(xf * xf, axis=-1, keepdims=True) + NORM_EPS)
    return (y * w.astype(jnp.float32)).astype(x.dtype)


def layer_norm(x, w, b):
    xf = x.astype(jnp.float32)
    mu = jnp.mean(xf, axis=-1, keepdims=True)
    xc = xf - mu
    y = xc * lax.rsqrt(jnp.mean(xc * xc, axis=-1, keepdims=True) + NORM_EPS)
    return (y * w.astype(jnp.float32) + b.astype(jnp.float32)).astype(x.dtype)


def swiglu(x, w1, w3, w2):
    return (jax.nn.silu(x @ w1) * (x @ w3)) @ w2


def depthwise_conv(x, w, b):
    c = x.shape[-1]
    y = lax.conv_general_dilated(
        x, w[:, None, :].astype(x.dtype), window_strides=(1,), padding=[(CONV_PAD, CONV_PAD)],
        dimension_numbers=('NWC', 'WIO', 'NWC'), feature_group_count=c)
    return y + b.astype(y.dtype)


def gla_scan(q, k, v, g):
    B, L, H, K = q.shape
    V = v.shape[-1]
    n = L // CHUNK

    def to_chunks(t):
        return t.reshape(B, n, CHUNK, H, t.shape[-1]).transpose(1, 0, 3, 2, 4)

    lower = jnp.tril(jnp.ones((CHUNK, CHUNK), dtype=bool))[:, :, None]

    def step(S, inp):
        qc, kc, vc, gc = inp
        b = jnp.cumsum(gc, axis=2)
        o_inter = jnp.einsum('bhtk,bhkv->bhtv', qc * jnp.exp(b), S)
        diff = jnp.where(lower, b[:, :, :, None, :] - b[:, :, None, :, :], -jnp.inf)
        scores = jnp.einsum('bhtk,bhtsk,bhsk->bhts', qc, jnp.exp(diff), kc)
        o_intra = jnp.einsum('bhts,bhsv->bhtv', scores, vc)
        b_last = b[:, :, -1, :]
        S = S * jnp.exp(b_last)[..., None] + jnp.einsum(
            'bhsk,bhsv->bhkv', kc * jnp.exp(b_last[:, :, None, :] - b), vc)
        return S, o_inter + o_intra

    S0 = jnp.zeros((B, H, K, V), jnp.float32)
    _, o = lax.scan(step, S0, (to_chunks(q), to_chunks(k), to_chunks(v), to_chunks(g)))
    return o.transpose(1, 0, 3, 2, 4).reshape(B, L, H, V)


def hgrn2_bidirectional(q_raw, i_raw, zf_fwd, zf_bwd, lb_f, lb_b):
    B, L, _ = q_raw.shape
    q = jax.nn.silu(q_raw.astype(jnp.float32)).reshape(B, L, HG_HEADS, HG_DK)
    v = i_raw.astype(jnp.float32).reshape(B, L, HG_HEADS, HG_DV)

    def gates(z, lb):
        f = lb + (1.0 - lb) * jax.nn.sigmoid(z.astype(jnp.float32))
        return ((1.0 - f).reshape(B, L, HG_HEADS, HG_DK),
                jnp.log(f).reshape(B, L, HG_HEADS, HG_DK))

    k_f, g_f = gates(zf_fwd, lb_f)
    k_b, g_b = gates(zf_bwd, lb_b)
    o_f = gla_scan(q, k_f, v, g_f)
    flip = lambda t: t[:, ::-1]
    o_b = flip(gla_scan(flip(q), flip(k_b), flip(v), flip(g_b)))
    return o_f + o_b


def encoder(x, ffn1_norm, ffn1_w1, ffn1_w3, ffn1_w2, mix_norm, w_in, conv_w, conv_b,
            conv_ln_w, conv_ln_b, lb_fwd, lb_bwd, hg_norm, w_out,
            ffn2_norm, ffn2_w1, ffn2_w3, ffn2_w2, final_norm):
    B, L, _ = x.shape
    lbf_all = jnp.cumsum(jax.nn.softmax(lb_fwd.astype(jnp.float32), axis=0), axis=0)
    lbb_all = jnp.cumsum(jax.nn.softmax(lb_bwd.astype(jnp.float32), axis=0), axis=0)
    for l in range(DEPTH):
        x = x + 0.5 * swiglu(rms_norm(x, ffn1_norm[l]), ffn1_w1[l], ffn1_w3[l], ffn1_w2[l])

        h = rms_norm(x, mix_norm[l])
        proj = h @ w_in[l]
        a, ga, q, zf, zb, iv, og = jnp.split(proj, SPLIT_IDX, axis=-1)

        u = a * jax.nn.sigmoid(ga)
        u = depthwise_conv(u, conv_w[l], conv_b[l])
        u = jax.nn.silu(layer_norm(u, conv_ln_w[l], conv_ln_b[l]))

        o = hgrn2_bidirectional(q, iv, zf, zb, lbf_all[l], lbb_all[l])
        o = o * lax.rsqrt(jnp.mean(o * o, axis=-1, keepdims=True) + NORM_EPS) * hg_norm[l].astype(jnp.float32)
        o = (o.reshape(B, L, HG_WIDTH) * jax.nn.silu(og.astype(jnp.float32))).astype(x.dtype)

        x = x + jnp.concatenate([u.astype(x.dtype), o], axis=-1) @ w_out[l]

        x = x + 0.5 * swiglu(rms_norm(x, ffn2_norm[l]), ffn2_w1[l], ffn2_w3[l], ffn2_w2[l])
    return rms_norm(x, final_norm)


def setup_inputs(seed: int = 0) -> dict:
    key = jax.random.key(seed)
    ks = jax.random.split(key, 24)
    f32 = jnp.float32
    nrm = lambda k, shape, scale: jax.random.normal(k, shape, f32) * scale
    gain = lambda k, shape: 1.0 + 0.02 * jax.random.normal(k, shape, f32)
    return {
        "x_prompt": nrm(ks[0], (BATCH, SEQ, D_MODEL), 1.0),
        "x_sample": nrm(ks[1], (DEC_BATCH, DEC_SEQ, D_MODEL), 1.0),
        "ffn1_norm": gain(ks[2], (DEPTH, D_MODEL)),
        "ffn1_w1": nrm(ks[3], (DEPTH, D_MODEL, D_FF), D_MODEL ** -0.5),
        "ffn1_w3": nrm(ks[4], (DEPTH, D_MODEL, D_FF), D_MODEL ** -0.5),
        "ffn1_w2": nrm(ks[5], (DEPTH, D_FF, D_MODEL), D_FF ** -0.5),
        "mix_norm": gain(ks[6], (DEPTH, D_MODEL)),
        "w_in": nrm(ks[7], (DEPTH, D_MODEL, IN_COLS), D_MODEL ** -0.5),
        "conv_w": nrm(ks[8], (DEPTH, CONV_KERNEL, CONV_WIDTH), CONV_KERNEL ** -0.5),
        "conv_b": nrm(ks[9], (DEPTH, CONV_WIDTH), 0.02),
        "conv_ln_w": gain(ks[10], (DEPTH, CONV_WIDTH)),
        "conv_ln_b": nrm(ks[11], (DEPTH, CONV_WIDTH), 0.02),
        "lb_fwd": nrm(ks[12], (DEPTH + 1, HG_WIDTH), 0.1),
        "lb_bwd": nrm(ks[13], (DEPTH + 1, HG_WIDTH), 0.1),
        "hg_norm": gain(ks[14], (DEPTH, HG_DV)),
        "w_out": nrm(ks[15], (DEPTH, MIX_WIDTH, D_MODEL), MIX_WIDTH ** -0.5),
        "ffn2_norm": gain(ks[16], (DEPTH, D_MODEL)),
        "ffn2_w1": nrm(ks[17], (DEPTH, D_MODEL, D_FF), D_MODEL ** -0.5),
        "ffn2_w3": nrm(ks[18], (DEPTH, D_MODEL, D_FF), D_MODEL ** -0.5),
        "ffn2_w2": nrm(ks[19], (DEPTH, D_FF, D_MODEL), D_FF ** -0.5),
        "final_norm": gain(ks[20], (D_MODEL,)),
    }


def reference(x_prompt, x_sample, ffn1_norm, ffn1_w1, ffn1_w3, ffn1_w2, mix_norm, w_in, conv_w, conv_b,
              conv_ln_w, conv_ln_b, lb_fwd, lb_bwd, hg_norm, w_out,
              ffn2_norm, ffn2_w1, ffn2_w3, ffn2_w2, final_norm):
    y_prompt = encoder(x_prompt, ffn1_norm, ffn1_w1, ffn1_w3, ffn1_w2, mix_norm, w_in, conv_w, conv_b,
                       conv_ln_w, conv_ln_b, lb_fwd, lb_bwd, hg_norm, w_out,
                       ffn2_norm, ffn2_w1, ffn2_w3, ffn2_w2, final_norm)
    y_sample = encoder(x_sample, ffn1_norm, ffn1_w1, ffn1_w3, ffn1_w2, mix_norm, w_in, conv_w, conv_b,
                       conv_ln_w, conv_ln_b, lb_fwd, lb_bwd, hg_norm, w_out,
                       ffn2_norm, ffn2_w1, ffn2_w3, ffn2_w2, final_norm)
    return (y_prompt, y_sample)
```

```python
import functools

import jax
import jax.numpy as jnp
from jax import lax
from jax.experimental import pallas as pl
from jax.experimental.pallas import tpu as pltpu

NORM_EPS = 1e-6
HEAD_DIM = 128
GLA_CHUNK = 64
CONV_HALO = 16
SAFE_LOG_DECAY = 60.0
V7X_VMEM_LIMIT = 56 * 1024 * 1024

F32 = jnp.float32
BF16 = jnp.bfloat16


def _sigmoid(x):
    return 1.0 / (1.0 + jnp.exp(-x))


def _rms(x, w):
    return x * lax.rsqrt(jnp.mean(x * x, axis=-1, keepdims=True) + NORM_EPS) * w


def _params(sem, vmem=V7X_VMEM_LIMIT):
    return pltpu.CompilerParams(dimension_semantics=sem, vmem_limit_bytes=vmem)


def _pick(n, prefs):
    for p in prefs:
        if n % p == 0:
            return p
    return n


def _ffn_body(x_ref, nw_ref, w1_ref, w3_ref, w2_ref, fn_ref, o_ref, h_scr, *, nj, final):
    j = pl.program_id(1)

    @pl.when(j == 0)
    def _():
        x = x_ref[...]
        h_scr[...] = _rms(x, nw_ref[...]).astype(BF16)
        o_ref[...] = x + x

    h = h_scr[...]
    a = jnp.dot(h, w1_ref[...], preferred_element_type=F32)
    b = jnp.dot(h, w3_ref[...], preferred_element_type=F32)
    p = (a * _sigmoid(a) * b).astype(BF16)
    o_ref[...] += jnp.dot(p, w2_ref[...], preferred_element_type=F32)

    @pl.when(j == nj - 1)
    def _():
        y = 0.5 * o_ref[...]
        if final:
            y = _rms(y, fn_ref[...])
        o_ref[...] = y


def _ffn(x, nw, w1, w3, w2, fn, *, final):
    n, d = x.shape
    f = w1.shape[1]
    tm = _pick(n, (512, 256, 128, 64, 32, 16, 8))
    tf = _pick(f, (512, 256, 128))
    nj = f // tf
    return pl.pallas_call(
        functools.partial(_ffn_body, nj=nj, final=final),
        grid=(n // tm, nj),
        in_specs=[
            pl.BlockSpec((tm, d), lambda i, j: (i, 0)),
            pl.BlockSpec((1, d), lambda i, j: (0, 0)),
            pl.BlockSpec((d, tf), lambda i, j: (0, j)),
            pl.BlockSpec((d, tf), lambda i, j: (0, j)),
            pl.BlockSpec((tf, d), lambda i, j: (j, 0)),
            pl.BlockSpec((1, d), lambda i, j: (0, 0)),
        ],
        out_specs=pl.BlockSpec((tm, d), lambda i, j: (i, 0)),
        out_shape=jax.ShapeDtypeStruct((n, d), F32),
        scratch_shapes=[pltpu.VMEM((tm, d), BF16)],
        compiler_params=_params(("parallel", "arbitrary")),
        name="ffn_final" if final else "ffn",
    )(x, nw, w1, w3, w2, fn)


def _lower_bound(lb_ref, layer):
    lb = lb_ref[...]
    e = jnp.exp(lb - jnp.max(lb, axis=0, keepdims=True))
    return jnp.sum(e[: layer + 1], axis=0, keepdims=True) / jnp.sum(e, axis=0, keepdims=True)


def _proj_body(x_ref, nw_ref, w_ref, lbf_ref, lbb_ref, o_ref, h_scr, a_scr, *, layer):
    j = pl.program_id(1)

    @pl.when(j == 0)
    def _():
        h_scr[...] = _rms(x_ref[...], nw_ref[...]).astype(BF16)

    r = jnp.dot(h_scr[...], w_ref[...], preferred_element_type=F32)

    @pl.when(j == 0)
    def _():
        a_scr[...] = r

    @pl.when(j == 1)
    def _():
        o_ref[...] = a_scr[...] * _sigmoid(r)

    @pl.when((j == 2) | (j == 6))
    def _():
        o_ref[...] = r * _sigmoid(r)

    def log_forget(lb_ref):
        lb = _lower_bound(lb_ref, layer)
        return jnp.log(lb + (1.0 - lb) * _sigmoid(r))

    @pl.when(j == 3)
    def _():
        o_ref[...] = log_forget(lbf_ref)

    @pl.when(j == 4)
    def _():
        o_ref[...] = log_forget(lbb_ref)

    @pl.when(j == 5)
    def _():
        o_ref[...] = r


def _proj(x, nw, w_in, lbf, lbb, *, layer):
    n, d = x.shape
    g = lbf.shape[1]
    assert w_in.shape[1] == 7 * g
    tm = _pick(n, (512, 256, 128, 64, 32, 16, 8))
    nl = lbf.shape[0]
    return pl.pallas_call(
        functools.partial(_proj_body, layer=layer),
        grid=(n // tm, 7),
        in_specs=[
            pl.BlockSpec((tm, d), lambda i, j: (i, 0)),
            pl.BlockSpec((1, d), lambda i, j: (0, 0)),
            pl.BlockSpec((d, g), lambda i, j: (0, j)),
            pl.BlockSpec((nl, g), lambda i, j: (0, 0)),
            pl.BlockSpec((nl, g), lambda i, j: (0, 0)),
        ],
        out_specs=pl.BlockSpec((tm, g), lambda i, j: (i, jnp.maximum(j - 1, 0))),
        out_shape=jax.ShapeDtypeStruct((n, 6 * g), F32),
        scratch_shapes=[pltpu.VMEM((tm, d), BF16), pltpu.VMEM((tm, g), F32)],
        compiler_params=_params(("parallel", "arbitrary")),
        name="in_proj",
    )(x, nw, w_in, lbf, lbb)


def _conv_body(c_ref, p_ref, n_ref, w_ref, cb_ref, lw_ref, lb_ref, o_ref, xp_scr, y_scr, *, tc, taps, rows):
    i = pl.program_id(1)
    ni = pl.num_programs(1)
    nb = xp_scr.shape[0]
    lane = HEAD_DIM
    first = CONV_HALO - taps // 2
    for k in range(nb):
        cols = slice(k * lane, (k + 1) * lane)
        xp_scr[k, pl.ds(CONV_HALO, tc), :] = c_ref[:, cols]
        xp_scr[k, pl.ds(0, CONV_HALO), :] = jnp.where(i > 0, p_ref[:, cols], 0.0)
        xp_scr[k, pl.ds(CONV_HALO + tc, CONV_HALO), :] = jnp.where(i < ni - 1, n_ref[:, cols], 0.0)

    nwin = (first + taps - 1) // 8 + 1

    def conv_chunk(idx, carry):
        k = idx // (tc // rows)
        base = pl.multiple_of((idx % (tc // rows)) * rows, rows)
        wins = [xp_scr[k, pl.ds(base + 8 * a, rows + 8), :] for a in range(nwin)]
        acc = jnp.zeros((rows, lane), F32) + cb_ref[k]
        for r in range(8):
            z = None
            for a in range(nwin):
                j = 8 * a + r - first
                if 0 <= j < taps:
                    term = w_ref[k, pl.ds(j, 1), :] * wins[a][r:r + rows, :]
                    z = term if z is None else z + term
            if z is not None:
                acc = acc + z
        y_scr[k, pl.ds(base, rows), :] = acc
        return carry

    lax.fori_loop(0, nb * (tc // rows), conv_chunk, 0)

    nrm_rows = 32
    inv_c = 1.0 / (nb * lane)

    def norm_chunk(ci, carry):
        base = pl.multiple_of(ci * nrm_rows, nrm_rows)
        y = y_scr[:, pl.ds(base, nrm_rows), :]
        mu = jnp.sum(jnp.sum(y, axis=0), axis=-1, keepdims=True) * inv_c
        yc = y - mu
        var = jnp.sum(jnp.sum(yc * yc, axis=0), axis=-1, keepdims=True) * inv_c
        yn = yc * lax.rsqrt(var + NORM_EPS) * lw_ref[...] + lb_ref[...]
        yn = (yn * _sigmoid(yn)).astype(o_ref.dtype)
        for k in range(nb):
            o_ref[pl.ds(base, nrm_rows), k * lane:(k + 1) * lane] = yn[k]
        return carry

    lax.fori_loop(0, tc // nrm_rows, norm_chunk, 0)


def _conv(proj, conv_w, conv_b, ln_w, ln_b, *, batch, seq):
    n = proj.shape[0]
    taps, cw = conv_w.shape
    assert taps // 2 <= CONV_HALO and n == batch * seq and cw % HEAD_DIM == 0
    nb = cw // HEAD_DIM
    tc = _pick(seq, (512, 256, 128, 64))
    nt = seq // tc
    hb = tc // CONV_HALO
    nhb = n // CONV_HALO
    blocked = lambda t: t.reshape(t.shape[0], nb, HEAD_DIM).transpose(1, 0, 2)
    whole = lambda shape: pl.BlockSpec(shape, lambda b, i: (0,) * len(shape))
    return pl.pallas_call(
        functools.partial(_conv_body, tc=tc, taps=taps, rows=64),
        grid=(batch, nt),
        in_specs=[
            pl.BlockSpec((tc, cw), lambda b, i: (b * nt + i, 0)),
            pl.BlockSpec((CONV_HALO, cw), lambda b, i: (jnp.maximum((b * nt + i) * hb - 1, 0), 0)),
            pl.BlockSpec((CONV_HALO, cw), lambda b, i: (jnp.minimum((b * nt + i + 1) * hb, nhb - 1), 0)),
            whole((nb, taps, HEAD_DIM)),
            whole((nb, 1, HEAD_DIM)),
            whole((nb, 1, HEAD_DIM)),
            whole((nb, 1, HEAD_DIM)),
        ],
        out_specs=pl.BlockSpec((tc, cw), lambda b, i: (b * nt + i, 0)),
        out_shape=jax.ShapeDtypeStruct((n, cw), BF16),
        scratch_shapes=[pltpu.VMEM((nb, tc + 2 * CONV_HALO, HEAD_DIM), F32), pltpu.VMEM((nb, tc, HEAD_DIM), F32)],
        compiler_params=_params(("parallel", "parallel")),
        name="conv_ln",
    )(proj, proj, proj, blocked(conv_w), blocked(conv_b), blocked(ln_w), blocked(ln_b))


def _split3(g):
    g1 = g.astype(BF16)
    r1 = g - g1.astype(F32)
    g2 = r1.astype(BF16)
    g3 = (r1 - g2.astype(F32)).astype(BF16)
    return g1, g2, g3


def _gla_chunk(q, g, v, st, tcum, mask, last):
    g1, g2, g3 = _split3(g)
    b = (jnp.dot(tcum, g1, preferred_element_type=F32) + jnp.dot(tcum, g2, preferred_element_type=F32)
         + jnp.dot(tcum, g3, preferred_element_type=F32))
    k = 1.0 - jnp.exp(g)
    qe = (q * jnp.exp(b)).astype(BF16)
    ke = (k * jnp.exp(-b)).astype(BF16)
    bl = b[last:last + 1, :]
    k2 = (k * jnp.exp(bl - b)).astype(BF16)
    vb = v.astype(BF16)
    nt = (((1,), (1,)), ((), ()))
    tn = (((0,), (0,)), ((), ()))
    o = lax.dot_general(qe, st.astype(BF16), nt, preferred_element_type=F32)
    sc = lax.dot_general(qe, ke, nt, preferred_element_type=F32)
    sc = jnp.where(mask, sc, 0.0).astype(BF16)
    o = o + jnp.dot(sc, vb, preferred_element_type=F32)
    st = st * jnp.exp(bl) + lax.dot_general(vb, k2, tn, preferred_element_type=F32)
    return o, st


def _gla_body(qf_ref, gf_ref, vf_ref, qb_ref, gb_ref, vb_ref, of_ref, ob_ref, sf_scr, sb_scr, *, tl):
    c = pl.program_id(2)

    @pl.when(c == 0)
    def _():
        sf_scr[...] = jnp.zeros_like(sf_scr)
        sb_scr[...] = jnp.zeros_like(sb_scr)

    ch = GLA_CHUNK
    row = lax.broadcasted_iota(jnp.int32, (ch, ch), 0)
    col = lax.broadcasted_iota(jnp.int32, (ch, ch), 1)
    lo = col <= row
    up = col >= row
    t_lo = jnp.where(lo, 1.0, 0.0).astype(BF16)
    t_up = jnp.where(up, 1.0, 0.0).astype(BF16)
    nch = tl // ch
    for ci in range(nch):
        sl = pl.ds(ci * ch, ch)
        o, st = _gla_chunk(qf_ref[sl, :], gf_ref[sl, :], vf_ref[sl, :], sf_scr[...], t_lo, lo, ch - 1)
        of_ref[sl, :] = o
        sf_scr[...] = st
        sl = pl.ds((nch - 1 - ci) * ch, ch)
        o, st = _gla_chunk(qb_ref[sl, :], gb_ref[sl, :], vb_ref[sl, :], sb_scr[...], t_up, up, 0)
        ob_ref[sl, :] = o
        sb_scr[...] = st


def _gla(proj, *, batch, seq, heads, group):
    n = proj.shape[0]
    tl = _pick(seq, (256, 128, 64))
    nt = seq // tl
    hb = group // HEAD_DIM
    assert hb == heads

    def fwd(grp):
        return pl.BlockSpec((tl, HEAD_DIM), lambda b, h, c: (b * nt + c, grp * hb + h))

    def bwd(grp):
        return pl.BlockSpec((tl, HEAD_DIM), lambda b, h, c: (b * nt + nt - 1 - c, grp * hb + h))

    out = jax.ShapeDtypeStruct((n, heads * HEAD_DIM), F32)
    return pl.pallas_call(
        functools.partial(_gla_body, tl=tl),
        grid=(batch, heads, nt),
        in_specs=[fwd(1), fwd(2), fwd(4), bwd(1), bwd(3), bwd(4)],
        out_specs=[pl.BlockSpec((tl, HEAD_DIM), lambda b, h, c: (b * nt + c, h)),
                   pl.BlockSpec((tl, HEAD_DIM), lambda b, h, c: (b * nt + nt - 1 - c, h))],
        out_shape=[out, out],
        scratch_shapes=[pltpu.VMEM((HEAD_DIM, HEAD_DIM), F32), pltpu.VMEM((HEAD_DIM, HEAD_DIM), F32)],
        compiler_params=_params(("parallel", "parallel", "arbitrary")),
        name="gla_scan",
    )(proj, proj, proj, proj, proj, proj)


def _mix_out_body(x_ref, u_ref, of_ref, ob_ref, og_ref, hn_ref, wu_ref, wo_ref, o_ref, *, heads):
    o = of_ref[...] + ob_ref[...]
    parts = []
    for h in range(heads):
        oh = o[:, h * HEAD_DIM:(h + 1) * HEAD_DIM]
        parts.append(_rms(oh, hn_ref[...]))
    on = (jnp.concatenate(parts, axis=-1) * og_ref[...]).astype(BF16)
    y = x_ref[...] + jnp.dot(u_ref[...], wu_ref[...], preferred_element_type=F32)
    o_ref[...] = y + jnp.dot(on, wo_ref[...], preferred_element_type=F32)


def _mix_out(x, u, o_f, o_b, proj, hn, w_u, w_o, *, heads, group):
    n, d = x.shape
    cw = u.shape[1]
    hw = o_f.shape[1]
    assert hw == group
    tm = _pick(n, (512, 256, 128, 64, 32, 16, 8))
    return pl.pallas_call(
        functools.partial(_mix_out_body, heads=heads),
        grid=(n // tm,),
        in_specs=[
            pl.BlockSpec((tm, d), lambda i: (i, 0)),
            pl.BlockSpec((tm, cw), lambda i: (i, 0)),
            pl.BlockSpec((tm, hw), lambda i: (i, 0)),
            pl.BlockSpec((tm, hw), lambda i: (i, 0)),
            pl.BlockSpec((tm, hw), lambda i: (i, 5)),
            pl.BlockSpec((1, HEAD_DIM), lambda i: (0, 0)),
            pl.BlockSpec((cw, d), lambda i: (0, 0)),
            pl.BlockSpec((hw, d), lambda i: (0, 0)),
        ],
        out_specs=pl.BlockSpec((tm, d), lambda i: (i, 0)),
        out_shape=jax.ShapeDtypeStruct((n, d), F32),
        compiler_params=_params(("parallel",)),
        name="mix_out",
    )(x, u, o_f, o_b, proj, hn, w_u, w_o)


def _encoder(x, p):
    batch, seq, d = x.shape
    x = x.reshape(batch * seq, d)
    depth = p["w_in"].shape[0]
    for l in range(depth):
        cw = p["conv_w"].shape[-1]
        hw = p["lb_fwd"].shape[-1]
        heads = hw // HEAD_DIM
        assert cw == hw and p["hg_norm"].shape[-1] == HEAD_DIM
        row = lambda t: t[l].reshape(1, -1)
        x = _ffn(x, row(p["ffn1_norm"]), p["ffn1_w1"][l], p["ffn1_w3"][l], p["ffn1_w2"][l],
                 row(p["ffn1_norm"]), final=False)
        proj = _proj(x, row(p["mix_norm"]), p["w_in"][l], p["lb_fwd"], p["lb_bwd"], layer=l)
        u = _conv(proj, p["conv_w"][l], row(p["conv_b"]), row(p["conv_ln_w"]), row(p["conv_ln_b"]),
                  batch=batch, seq=seq)
        o_f, o_b = _gla(proj, batch=batch, seq=seq, heads=heads, group=hw)
        x = _mix_out(x, u, o_f, o_b, proj, row(p["hg_norm"]), p["w_out"][l][:cw], p["w_out"][l][cw:],
                     heads=heads, group=hw)
        last = l == depth - 1
        fn = p["final_norm"].reshape(1, -1)
        x = _ffn(x, row(p["ffn2_norm"]), p["ffn2_w1"][l], p["ffn2_w3"][l], p["ffn2_w2"][l], fn, final=last)
    if depth == 0:
        raise NotImplementedError("depth 0")
    return x.reshape(batch, seq, d)


_MATMUL_WEIGHTS = ("ffn1_w1", "ffn1_w3", "ffn1_w2", "w_in", "w_out", "ffn2_w1", "ffn2_w3", "ffn2_w2")


def kernel(x_prompt, x_sample, ffn1_norm, ffn1_w1, ffn1_w3, ffn1_w2, mix_norm, w_in, conv_w, conv_b, conv_ln_w,
           conv_ln_b, lb_fwd, lb_bwd, hg_norm, w_out, ffn2_norm, ffn2_w1, ffn2_w3, ffn2_w2, final_norm):
    p = dict(ffn1_norm=ffn1_norm, ffn1_w1=ffn1_w1, ffn1_w3=ffn1_w3, ffn1_w2=ffn1_w2, mix_norm=mix_norm, w_in=w_in,
             conv_w=conv_w, conv_b=conv_b, conv_ln_w=conv_ln_w, conv_ln_b=conv_ln_b, lb_fwd=lb_fwd, lb_bwd=lb_bwd,
             hg_norm=hg_norm, w_out=w_out, ffn2_norm=ffn2_norm, ffn2_w1=ffn2_w1, ffn2_w3=ffn2_w3, ffn2_w2=ffn2_w2,
             final_norm=final_norm)
    for name in _MATMUL_WEIGHTS:
        p[name] = p[name].astype(BF16)
    return _encoder(x_prompt, p), _encoder(x_sample, p)
```

```python
import functools

import jax
import jax.numpy as jnp
from jax import lax
from jax.experimental import pallas as pl
from jax.experimental.pallas import tpu as pltpu

NORM_EPS = 1e-6
HEAD_DIM = 128
GLA_CHUNK = 64
CONV_HALO = 16
SAFE_LOG_DECAY = 60.0
V7X_VMEM_LIMIT = 56 * 1024 * 1024

F32 = jnp.float32
BF16 = jnp.bfloat16


def _sigmoid(x):
    return 1.0 / (1.0 + jnp.exp(-x))


def _rms(x, w):
    return x * lax.rsqrt(jnp.mean(x * x, axis=-1, keepdims=True) + NORM_EPS) * w


def _params(sem, vmem=V7X_VMEM_LIMIT):
    return pltpu.CompilerParams(dimension_semantics=sem, vmem_limit_bytes=vmem)


def _pick(n, prefs):
    for p in prefs:
        if n % p == 0:
            return p
    return n


def _ffn_body(x_ref, nw_ref, w1_ref, w3_ref, w2_ref, fn_ref, o_ref, h_scr, *, nj, final):
    j = pl.program_id(1)

    @pl.when(j == 0)
    def _():
        x = x_ref[...]
        h_scr[...] = _rms(x, nw_ref[...]).astype(BF16)
        o_ref[...] = x + x

    h = h_scr[...]
    a = jnp.dot(h, w1_ref[...], preferred_element_type=F32)
    b = jnp.dot(h, w3_ref[...], preferred_element_type=F32)
    p = (a * _sigmoid(a) * b).astype(BF16)
    o_ref[...] += jnp.dot(p, w2_ref[...], preferred_element_type=F32)

    @pl.when(j == nj - 1)
    def _():
        y = 0.5 * o_ref[...]
        if final:
            y = _rms(y, fn_ref[...])
        o_ref[...] = y


def _ffn(x, nw, w1, w3, w2, fn, *, final):
    n, d = x.shape
    f = w1.shape[1]
    tm = _pick(n, (512, 256, 128, 64, 32, 16, 8))
    tf = _pick(f, (512, 256, 128))
    nj = f // tf
    return pl.pallas_call(
        functools.partial(_ffn_body, nj=nj, final=final),
        grid=(n // tm, nj),
        in_specs=[
            pl.BlockSpec((tm, d), lambda i, j: (i, 0)),
            pl.BlockSpec((1, d), lambda i, j: (0, 0)),
            pl.BlockSpec((d, tf), lambda i, j: (0, j)),
            pl.BlockSpec((d, tf), lambda i, j: (0, j)),
            pl.BlockSpec((tf, d), lambda i, j: (j, 0)),
            pl.BlockSpec((1, d), lambda i, j: (0, 0)),
        ],
        out_specs=pl.BlockSpec((tm, d), lambda i, j: (i, 0)),
        out_shape=jax.ShapeDtypeStruct((n, d), F32),
        scratch_shapes=[pltpu.VMEM((tm, d), BF16)],
        compiler_params=_params(("parallel", "arbitrary")),
        name="ffn_final" if final else "ffn",
    )(x, nw, w1, w3, w2, fn)


def _lower_bound(lb_ref, layer):
    lb = lb_ref[...]
    e = jnp.exp(lb - jnp.max(lb, axis=0, keepdims=True))
    return jnp.sum(e[: layer + 1], axis=0, keepdims=True) / jnp.sum(e, axis=0, keepdims=True)


def _proj_body(x_ref, nw_ref, w_ref, lbf_ref, lbb_ref, o_ref, *, layer, g):
    h = _rms(x_ref[...], nw_ref[...]).astype(BF16)

    def group(j):
        return jnp.dot(h, w_ref[:, j * g:(j + 1) * g], preferred_element_type=F32)

    def put(j, val):
        o_ref[:, j * g:(j + 1) * g] = val

    put(0, group(0) * _sigmoid(group(1)))
    q = group(2)
    put(1, q * _sigmoid(q))
    for j_out, j_in, lb_ref in ((2, 3, lbf_ref), (3, 4, lbb_ref)):
        lb = _lower_bound(lb_ref, layer)
        put(j_out, jnp.log(lb + (1.0 - lb) * _sigmoid(group(j_in))))
    put(4, group(5))
    og = group(6)
    put(5, og * _sigmoid(og))


def _proj(x, nw, w_in, lbf, lbb, *, layer):
    n, d = x.shape
    g = lbf.shape[1]
    assert w_in.shape[1] == 7 * g
    tm = _pick(n, (256, 128, 64, 32, 16, 8))
    nl = lbf.shape[0]
    once = pl.Buffered(1)
    return pl.pallas_call(
        functools.partial(_proj_body, layer=layer, g=g),
        grid=(n // tm,),
        in_specs=[
            pl.BlockSpec((tm, d), lambda i: (i, 0)),
            pl.BlockSpec((1, d), lambda i: (0, 0)),
            pl.BlockSpec((d, 7 * g), lambda i: (0, 0), pipeline_mode=once),
            pl.BlockSpec((nl, g), lambda i: (0, 0)),
            pl.BlockSpec((nl, g), lambda i: (0, 0)),
        ],
        out_specs=pl.BlockSpec((tm, 6 * g), lambda i: (i, 0)),
        out_shape=jax.ShapeDtypeStruct((n, 6 * g), F32),
        compiler_params=_params(("parallel",)),
        name="in_proj",
    )(x, nw, w_in, lbf, lbb)


def _conv_body(c_ref, p_ref, n_ref, w_ref, cb_ref, lw_ref, lb_ref, o_ref, xp_scr, y_scr, *, tc, taps, rows):
    i = pl.program_id(1)
    ni = pl.num_programs(1)
    nb = xp_scr.shape[0]
    lane = HEAD_DIM
    first = CONV_HALO - taps // 2
    for k in range(nb):
        cols = slice(k * lane, (k + 1) * lane)
        xp_scr[k, pl.ds(CONV_HALO, tc), :] = c_ref[:, cols]
        xp_scr[k, pl.ds(0, CONV_HALO), :] = jnp.where(i > 0, p_ref[:, cols], 0.0)
        xp_scr[k, pl.ds(CONV_HALO + tc, CONV_HALO), :] = jnp.where(i < ni - 1, n_ref[:, cols], 0.0)

    nwin = (first + taps - 1) // 8 + 1

    def conv_chunk(idx, carry):
        k = idx // (tc // rows)
        base = pl.multiple_of((idx % (tc // rows)) * rows, rows)
        wins = [xp_scr[k, pl.ds(base + 8 * a, rows + 8), :] for a in range(nwin)]
        acc = jnp.zeros((rows, lane), F32) + cb_ref[k]
        for r in range(8):
            z = None
            for a in range(nwin):
                j = 8 * a + r - first
                if 0 <= j < taps:
                    term = w_ref[k, pl.ds(j, 1), :] * wins[a][r:r + rows, :]
                    z = term if z is None else z + term
            if z is not None:
                acc = acc + z
        y_scr[k, pl.ds(base, rows), :] = acc
        return carry

    lax.fori_loop(0, nb * (tc // rows), conv_chunk, 0)

    nrm_rows = 32
    inv_c = 1.0 / (nb * lane)

    def norm_chunk(ci, carry):
        base = pl.multiple_of(ci * nrm_rows, nrm_rows)
        y = y_scr[:, pl.ds(base, nrm_rows), :]
        mu = jnp.sum(jnp.sum(y, axis=0), axis=-1, keepdims=True) * inv_c
        yc = y - mu
        var = jnp.sum(jnp.sum(yc * yc, axis=0), axis=-1, keepdims=True) * inv_c
        yn = yc * lax.rsqrt(var + NORM_EPS) * lw_ref[...] + lb_ref[...]
        yn = (yn * _sigmoid(yn)).astype(o_ref.dtype)
        for k in range(nb):
            o_ref[pl.ds(base, nrm_rows), k * lane:(k + 1) * lane] = yn[k]
        return carry

    lax.fori_loop(0, tc // nrm_rows, norm_chunk, 0, unroll=4)


def _conv(proj, conv_w, conv_b, ln_w, ln_b, *, batch, seq):
    n = proj.shape[0]
    taps, cw = conv_w.shape
    assert taps // 2 <= CONV_HALO and n == batch * seq and cw % HEAD_DIM == 0
    nb = cw // HEAD_DIM
    tc = _pick(seq, (512, 256, 128, 64))
    nt = seq // tc
    hb = tc // CONV_HALO
    nhb = n // CONV_HALO
    blocked = lambda t: t.reshape(t.shape[0], nb, HEAD_DIM).transpose(1, 0, 2)
    whole = lambda shape: pl.BlockSpec(shape, lambda b, i: (0,) * len(shape))
    return pl.pallas_call(
        functools.partial(_conv_body, tc=tc, taps=taps, rows=64),
        grid=(batch, nt),
        in_specs=[
            pl.BlockSpec((tc, cw), lambda b, i: (b * nt + i, 0)),
            pl.BlockSpec((CONV_HALO, cw), lambda b, i: (jnp.maximum((b * nt + i) * hb - 1, 0), 0)),
            pl.BlockSpec((CONV_HALO, cw), lambda b, i: (jnp.minimum((b * nt + i + 1) * hb, nhb - 1), 0)),
            whole((nb, taps, HEAD_DIM)),
            whole((nb, 1, HEAD_DIM)),
            whole((nb, 1, HEAD_DIM)),
            whole((nb, 1, HEAD_DIM)),
        ],
        out_specs=pl.BlockSpec((tc, cw), lambda b, i: (b * nt + i, 0)),
        out_shape=jax.ShapeDtypeStruct((n, cw), BF16),
        scratch_shapes=[pltpu.VMEM((nb, tc + 2 * CONV_HALO, HEAD_DIM), F32), pltpu.VMEM((nb, tc, HEAD_DIM), F32)],
        compiler_params=_params(("parallel", "parallel")),
        name="conv_ln",
    )(proj, proj, proj, blocked(conv_w), blocked(conv_b), blocked(ln_w), blocked(ln_b))


def _split3(g):
    def head(x):
        bits = pltpu.bitcast(x, jnp.uint32) & jnp.uint32(0xFFFF0000)
        return pltpu.bitcast(bits, F32)
    g1 = head(g)
    r1 = g - g1
    g2 = head(r1)
    g3 = r1 - g2
    return g1.astype(BF16), g2.astype(BF16), g3.astype(BF16)


_NT = (((1,), (1,)), ((), ()))
_TN = (((0,), (0,)), ((), ()))


def _gla_direction(q_ref, g_ref, v_ref, s_scr, tcum, last, order):
    ch = GLA_CHUNK
    g = g_ref[...]
    g1, g2, g3 = _split3(g)
    pieces = [p[i * ch:(i + 1) * ch] for i in order for p in (g1, g2, g3)]
    cs = jnp.dot(tcum, jnp.concatenate(pieces, axis=1), preferred_element_type=F32)
    chunks = []
    worst = None
    for n, i in enumerate(order):
        rows = slice(i * ch, (i + 1) * ch)
        c0 = 3 * HEAD_DIM * n
        b = cs[:, c0:c0 + HEAD_DIM] + cs[:, c0 + HEAD_DIM:c0 + 2 * HEAD_DIM] + cs[:, c0 + 2 * HEAD_DIM:c0 + 3 * HEAD_DIM]
        k = 1.0 - jnp.exp(g[rows])
        bl = b[last:last + 1, :]
        worst = bl if worst is None else jnp.minimum(worst, bl)
        chunks.append(dict(rows=rows, b=b, k=k, bl=bl, qe=(q_ref[rows, :] * jnp.exp(b)).astype(BF16),
                           vb=v_ref[rows, :].astype(BF16)))
    st = s_scr[...]
    for c in chunks:
        k2 = (c["k"] * jnp.exp(c["bl"] - c["b"])).astype(BF16)
        c["prior"] = st.astype(BF16)
        st = st * jnp.exp(c["bl"]) + lax.dot_general(c["vb"], k2, _TN, preferred_element_type=F32)
    s_scr[...] = st
    return chunks, jnp.min(worst)


def _gla_pairwise(q_ref, v_ref, k_scr, b_scr, o_ref, n_chunks, forward):
    ch = GLA_CHUNK
    t_idx = lax.broadcasted_iota(jnp.int32, (ch, 1), 0)

    def one_chunk(i, carry):
        rows = pl.ds(pl.multiple_of(i * ch, ch), ch)
        q = q_ref[rows, :]
        v = v_ref[rows, :]
        k = k_scr[rows, :]
        b = b_scr[rows, :]
        acc = jnp.zeros((ch, HEAD_DIM), F32)
        for s in range(ch):
            e = jnp.exp(jnp.minimum(b - b[s:s + 1, :], 0.0))
            w = jnp.sum(q * k[s:s + 1, :] * e, axis=-1, keepdims=True)
            seen = (t_idx >= s) if forward else (t_idx <= s)
            acc = acc + jnp.where(seen, w, 0.0) * v[s:s + 1, :]
        o_ref[rows, :] += acc
        return carry

    lax.fori_loop(0, n_chunks, one_chunk, 0)


def _gla_body(qf_ref, gf_ref, vf_ref, qb_ref, gb_ref, vb_ref, of_ref, ob_ref,
              sf_scr, sb_scr, kf_scr, bf_scr, kb_scr, bb_scr, *, tl):
    c = pl.program_id(2)

    @pl.when(c == 0)
    def _():
        sf_scr[...] = jnp.zeros_like(sf_scr)
        sb_scr[...] = jnp.zeros_like(sb_scr)

    ch = GLA_CHUNK
    nch = tl // ch
    row = lax.broadcasted_iota(jnp.int32, (ch, ch), 0)
    col = lax.broadcasted_iota(jnp.int32, (ch, ch), 1)
    lo = col <= row
    up = col >= row
    t_lo = jnp.where(lo, 1.0, 0.0).astype(BF16)
    t_up = jnp.where(up, 1.0, 0.0).astype(BF16)

    fwd, worst_f = _gla_direction(qf_ref, gf_ref, vf_ref, sf_scr, t_lo, ch - 1, list(range(nch)))
    bwd, worst_b = _gla_direction(qb_ref, gb_ref, vb_ref, sb_scr, t_up, 0, list(range(nch))[::-1])
    safe = jnp.minimum(worst_f, worst_b) >= -SAFE_LOG_DECAY
    work = [(c, of_ref, lo) for c in fwd] + [(c, ob_ref, up) for c in bwd]

    def inter(c):
        return lax.dot_general(c["qe"], c["prior"], _NT, preferred_element_type=F32)

    @pl.when(safe)
    def _():
        scores = []
        for c, _, _ in work:
            ke = (c["k"] * jnp.exp(-c["b"])).astype(BF16)
            scores.append(lax.dot_general(c["qe"], ke, _NT, preferred_element_type=F32))
        inters = [inter(c) for c, _, _ in work]
        probs = [jnp.where(mask, sc, 0.0).astype(BF16) for sc, (_, _, mask) in zip(scores, work)]
        for p, o_in, (c, o_ref, _) in zip(probs, inters, work):
            o_ref[c["rows"], :] = o_in + jnp.dot(p, c["vb"], preferred_element_type=F32)

    @pl.when(jnp.logical_not(safe))
    def _():
        for c, o_ref, _ in work:
            o_ref[c["rows"], :] = inter(c)
        for chunks, k_scr, b_scr in ((fwd, kf_scr, bf_scr), (bwd, kb_scr, bb_scr)):
            for c in chunks:
                k_scr[c["rows"], :] = c["k"]
                b_scr[c["rows"], :] = c["b"]
        _gla_pairwise(qf_ref, vf_ref, kf_scr, bf_scr, of_ref, nch, True)
        _gla_pairwise(qb_ref, vb_ref, kb_scr, bb_scr, ob_ref, nch, False)


def _gla(proj, *, batch, seq, heads, group):
    n = proj.shape[0]
    tl = _pick(seq, (512, 256, 128, 64))
    nt = seq // tl
    hb = group // HEAD_DIM
    assert hb == heads

    def fwd(grp):
        return pl.BlockSpec((tl, HEAD_DIM), lambda b, h, c: (b * nt + c, grp * hb + h))

    def bwd(grp):
        return pl.BlockSpec((tl, HEAD_DIM), lambda b, h, c: (b * nt + nt - 1 - c, grp * hb + h))

    out = jax.ShapeDtypeStruct((n, heads * HEAD_DIM), F32)
    return pl.pallas_call(
        functools.partial(_gla_body, tl=tl),
        grid=(batch, heads, nt),
        in_specs=[fwd(1), fwd(2), fwd(4), bwd(1), bwd(3), bwd(4)],
        out_specs=[pl.BlockSpec((tl, HEAD_DIM), lambda b, h, c: (b * nt + c, h)),
                   pl.BlockSpec((tl, HEAD_DIM), lambda b, h, c: (b * nt + nt - 1 - c, h))],
        out_shape=[out, out],
        scratch_shapes=[pltpu.VMEM((HEAD_DIM, HEAD_DIM), F32)] * 2 + [pltpu.VMEM((tl, HEAD_DIM), F32)] * 4,
        compiler_params=_params(("parallel", "parallel", "arbitrary")),
        name="gla_scan",
    )(proj, proj, proj, proj, proj, proj)


def _mix_out_body(x_ref, u_ref, of_ref, ob_ref, og_ref, hn_ref, wu_ref, wo_ref, o_ref, *, heads):
    o = of_ref[...] + ob_ref[...]
    parts = []
    for h in range(heads):
        oh = o[:, h * HEAD_DIM:(h + 1) * HEAD_DIM]
        parts.append(_rms(oh, hn_ref[...]))
    on = (jnp.concatenate(parts, axis=-1) * og_ref[...]).astype(BF16)
    y = x_ref[...] + jnp.dot(u_ref[...], wu_ref[...], preferred_element_type=F32)
    o_ref[...] = y + jnp.dot(on, wo_ref[...], preferred_element_type=F32)


def _mix_out(x, u, o_f, o_b, proj, hn, w_u, w_o, *, heads, group):
    n, d = x.shape
    cw = u.shape[1]
    hw = o_f.shape[1]
    assert hw == group
    tm = _pick(n, (512, 256, 128, 64, 32, 16, 8))
    return pl.pallas_call(
        functools.partial(_mix_out_body, heads=heads),
        grid=(n // tm,),
        in_specs=[
            pl.BlockSpec((tm, d), lambda i: (i, 0)),
            pl.BlockSpec((tm, cw), lambda i: (i, 0)),
            pl.BlockSpec((tm, hw), lambda i: (i, 0)),
            pl.BlockSpec((tm, hw), lambda i: (i, 0)),
            pl.BlockSpec((tm, hw), lambda i: (i, 5)),
            pl.BlockSpec((1, HEAD_DIM), lambda i: (0, 0)),
            pl.BlockSpec((cw, d), lambda i: (0, 0)),
            pl.BlockSpec((hw, d), lambda i: (0, 0)),
        ],
        out_specs=pl.BlockSpec((tm, d), lambda i: (i, 0)),
        out_shape=jax.ShapeDtypeStruct((n, d), F32),
        compiler_params=_params(("parallel",)),
        name="mix_out",
    )(x, u, o_f, o_b, proj, hn, w_u, w_o)


def _encoder(x, p):
    batch, seq, d = x.shape
    x = x.reshape(batch * seq, d)
    depth = p["w_in"].shape[0]
    for l in range(depth):
        cw = p["conv_w"].shape[-1]
        hw = p["lb_fwd"].shape[-1]
        heads = hw // HEAD_DIM
        assert cw == hw and p["hg_norm"].shape[-1] == HEAD_DIM
        row = lambda t: t[l].reshape(1, -1)
        x = _ffn(x, row(p["ffn1_norm"]), p["ffn1_w1"][l], p["ffn1_w3"][l], p["ffn1_w2"][l],
                 row(p["ffn1_norm"]), final=False)
        proj = _proj(x, row(p["mix_norm"]), p["w_in"][l], p["lb_fwd"], p["lb_bwd"], layer=l)
        u = _conv(proj, p["conv_w"][l], row(p["conv_b"]), row(p["conv_ln_w"]), row(p["conv_ln_b"]),
                  batch=batch, seq=seq)
        o_f, o_b = _gla(proj, batch=batch, seq=seq, heads=heads, group=hw)
        x = _mix_out(x, u, o_f, o_b, proj, row(p["hg_norm"]), p["w_out"][l][:cw], p["w_out"][l][cw:],
                     heads=heads, group=hw)
        last = l == depth - 1
        fn = p["final_norm"].reshape(1, -1)
        x = _ffn(x, row(p["ffn2_norm"]), p["ffn2_w1"][l], p["ffn2_w3"][l], p["ffn2_w2"][l], fn, final=last)
    if depth == 0:
        raise NotImplementedError("depth 0")
    return x.reshape(batch, seq, d)


_MATMUL_WEIGHTS = ("ffn1_w1", "ffn1_w3", "ffn1_w2", "w_in", "w_out", "ffn2_w1", "ffn2_w3", "ffn2_w2")


def kernel(x_prompt, x_sample, ffn1_norm, ffn1_w1, ffn1_w3, ffn1_w2, mix_norm, w_in, conv_w, conv_b, conv_ln_w,
           conv_ln_b, lb_fwd, lb_bwd, hg_norm, w_out, ffn2_norm, ffn2_w1, ffn2_w3, ffn2_w2, final_norm):
    p = dict(ffn1_norm=ffn1_norm, ffn1_w1=ffn1_w1, ffn1_w3=ffn1_w3, ffn1_w2=ffn1_w2, mix_norm=mix_norm, w_in=w_in,
             conv_w=conv_w, conv_b=conv_b, conv_ln_w=conv_ln_w, conv_ln_b=conv_ln_b, lb_fwd=lb_fwd, lb_bwd=lb_bwd,
             hg_norm=hg_norm, w_out=w_out, ffn2_norm=ffn2_norm, ffn2_w1=ffn2_w1, ffn2_w3=ffn2_w3, ffn2_w2=ffn2_w2,
             final_norm=final_norm)
    for name in _MATMUL_WEIGHTS:
        p[name] = p[name].astype(BF16)
    return _encoder(x_prompt, p), _encoder(x_sample, p)
```

```python
import functools

import jax
import jax.numpy as jnp
from jax import lax
from jax.experimental import pallas as pl
from jax.experimental.pallas import tpu as pltpu

NORM_EPS = 1e-6
HEAD_DIM = 128
GLA_CHUNK = 64
CONV_HALO = 16
SAFE_LOG_DECAY = 60.0
V7X_VMEM_LIMIT = 56 * 1024 * 1024
V7X_VMEM_LIMIT_FFN = 60 * 1024 * 1024

F32 = jnp.float32
BF16 = jnp.bfloat16


def _sigmoid(x):
    return 1.0 / (1.0 + jnp.exp(-x))


def _rms(x, w):
    return x * lax.rsqrt(jnp.mean(x * x, axis=-1, keepdims=True) + NORM_EPS) * w


def _params(sem, vmem=V7X_VMEM_LIMIT):
    return pltpu.CompilerParams(dimension_semantics=sem, vmem_limit_bytes=vmem)


def _pick(n, prefs):
    for p in prefs:
        if n % p == 0:
            return p
    return n


def _ffn_body(x_ref, nw_ref, w1_ref, w3_ref, w2_ref, fn_ref, o_ref, h_scr, *, nj, final, sub):
    j = pl.program_id(1)

    @pl.when(j == 0)
    def _():
        x = x_ref[...]
        h_scr[...] = _rms(x, nw_ref[...]).astype(BF16)
        o_ref[...] = x + x

    for r0 in range(0, h_scr.shape[0], sub):
        rows = slice(r0, r0 + sub)
        h = h_scr[rows, :]
        a = jnp.dot(h, w1_ref[...], preferred_element_type=F32)
        b = jnp.dot(h, w3_ref[...], preferred_element_type=F32)
        p = (a * _sigmoid(a) * b).astype(BF16)
        o_ref[rows, :] += jnp.dot(p, w2_ref[...], preferred_element_type=F32)

    @pl.when(j == nj - 1)
    def _():
        y = 0.5 * o_ref[...]
        if final:
            y = _rms(y, fn_ref[...])
        o_ref[...] = y


def _ffn(x, nw, w1, w3, w2, fn, *, final):
    n, d = x.shape
    f = w1.shape[1]
    tm = _pick(n, (1024, 512, 256, 128, 64, 32, 16, 8))
    tf = _pick(f, (512, 256, 128))
    nj = f // tf
    return pl.pallas_call(
        functools.partial(_ffn_body, nj=nj, final=final, sub=min(tm, 512)),
        grid=(n // tm, nj),
        in_specs=[
            pl.BlockSpec((tm, d), lambda i, j: (i, 0)),
            pl.BlockSpec((1, d), lambda i, j: (0, 0)),
            pl.BlockSpec((d, tf), lambda i, j: (0, j)),
            pl.BlockSpec((d, tf), lambda i, j: (0, j)),
            pl.BlockSpec((tf, d), lambda i, j: (j, 0)),
            pl.BlockSpec((1, d), lambda i, j: (0, 0)),
        ],
        out_specs=pl.BlockSpec((tm, d), lambda i, j: (i, 0)),
        out_shape=jax.ShapeDtypeStruct((n, d), F32),
        scratch_shapes=[pltpu.VMEM((tm, d), BF16)],
        compiler_params=_params(("parallel", "arbitrary"), V7X_VMEM_LIMIT_FFN),
        name="ffn_final" if final else "ffn",
    )(x, nw, w1, w3, w2, fn)


def _lower_bound(lb_ref, layer):
    lb = lb_ref[...]
    e = jnp.exp(lb - jnp.max(lb, axis=0, keepdims=True))
    return jnp.sum(e[: layer + 1], axis=0, keepdims=True) / jnp.sum(e, axis=0, keepdims=True)


def _proj_body(x_ref, nw_ref, w_ref, lbf_ref, lbb_ref, o_ref, *, layer, g):
    h = _rms(x_ref[...], nw_ref[...]).astype(BF16)

    def group(j):
        return jnp.dot(h, w_ref[:, j * g:(j + 1) * g], preferred_element_type=F32)

    def put(j, val):
        o_ref[:, j * g:(j + 1) * g] = val

    put(0, group(0) * _sigmoid(group(1)))
    q = group(2)
    put(1, q * _sigmoid(q))
    for j_out, j_in, lb_ref in ((2, 3, lbf_ref), (3, 4, lbb_ref)):
        lb = _lower_bound(lb_ref, layer)
        put(j_out, jnp.log(lb + (1.0 - lb) * _sigmoid(group(j_in))))
    put(4, group(5))
    og = group(6)
    put(5, og * _sigmoid(og))


def _proj(x, nw, w_in, lbf, lbb, *, layer):
    n, d = x.shape
    g = lbf.shape[1]
    assert w_in.shape[1] == 7 * g
    tm = _pick(n, (256, 128, 64, 32, 16, 8))
    nl = lbf.shape[0]
    once = pl.Buffered(1)
    return pl.pallas_call(
        functools.partial(_proj_body, layer=layer, g=g),
        grid=(n // tm,),
        in_specs=[
            pl.BlockSpec((tm, d), lambda i: (i, 0)),
            pl.BlockSpec((1, d), lambda i: (0, 0)),
            pl.BlockSpec((d, 7 * g), lambda i: (0, 0), pipeline_mode=once),
            pl.BlockSpec((nl, g), lambda i: (0, 0)),
            pl.BlockSpec((nl, g), lambda i: (0, 0)),
        ],
        out_specs=pl.BlockSpec((tm, 6 * g), lambda i: (i, 0)),
        out_shape=jax.ShapeDtypeStruct((n, 6 * g), F32),
        compiler_params=_params(("parallel",)),
        name="in_proj",
    )(x, nw, w_in, lbf, lbb)


def _split3(g):
    def head(x):
        bits = pltpu.bitcast(x, jnp.uint32) & jnp.uint32(0xFFFF0000)
        return pltpu.bitcast(bits, F32)
    g1 = head(g)
    r1 = g - g1
    g2 = head(r1)
    g3 = r1 - g2
    return g1.astype(BF16), g2.astype(BF16), g3.astype(BF16)


_NT = (((1,), (1,)), ((), ()))
_TN = (((0,), (0,)), ((), ()))


def _gla_direction(q_ref, g_ref, v_ref, s_scr, tcum, last, order):
    ch = GLA_CHUNK
    g = g_ref[...]
    g1, g2, g3 = _split3(g)
    pieces = [p[i * ch:(i + 1) * ch] for i in order for p in (g1, g2, g3)]
    cs = jnp.dot(tcum, jnp.concatenate(pieces, axis=1), preferred_element_type=F32)
    chunks = []
    worst = None
    for n, i in enumerate(order):
        rows = slice(i * ch, (i + 1) * ch)
        c0 = 3 * HEAD_DIM * n
        b = cs[:, c0:c0 + HEAD_DIM] + cs[:, c0 + HEAD_DIM:c0 + 2 * HEAD_DIM] + cs[:, c0 + 2 * HEAD_DIM:c0 + 3 * HEAD_DIM]
        k = 1.0 - jnp.exp(g[rows])
        bl = b[last:last + 1, :]
        worst = bl if worst is None else jnp.minimum(worst, bl)
        chunks.append(dict(rows=rows, b=b, k=k, bl=bl, qe=(q_ref[rows, :] * jnp.exp(b)).astype(BF16),
                           vb=v_ref[rows, :].astype(BF16)))
    st = s_scr[...]
    for c in chunks:
        k2 = (c["k"] * jnp.exp(c["bl"] - c["b"])).astype(BF16)
        c["prior"] = st.astype(BF16)
        st = st * jnp.exp(c["bl"]) + lax.dot_general(c["vb"], k2, _TN, preferred_element_type=F32)
    s_scr[...] = st
    return chunks, jnp.min(worst)


def _gla_pairwise(q_ref, v_ref, k_scr, b_scr, i_scr, o_ref, n_chunks, forward):
    ch = GLA_CHUNK
    t_idx = lax.broadcasted_iota(jnp.int32, (ch, 1), 0)

    def one_chunk(i, carry):
        rows = pl.ds(pl.multiple_of(i * ch, ch), ch)
        q = q_ref[rows, :]
        v = v_ref[rows, :]
        k = k_scr[rows, :]
        b = b_scr[rows, :]
        acc = jnp.zeros((ch, HEAD_DIM), F32)
        for s in range(ch):
            e = jnp.exp(jnp.minimum(b - b[s:s + 1, :], 0.0))
            w = jnp.sum(q * k[s:s + 1, :] * e, axis=-1, keepdims=True)
            seen = (t_idx >= s) if forward else (t_idx <= s)
            acc = acc + jnp.where(seen, w, 0.0) * v[s:s + 1, :]
        o_ref[rows, :] = (i_scr[rows, :] + acc).astype(o_ref.dtype)
        return carry

    lax.fori_loop(0, n_chunks, one_chunk, 0)


def _gla_body(qf_ref, gf_ref, vf_ref, qb_ref, gb_ref, vb_ref, of_ref, ob_ref,
              sf_scr, sb_scr, kf_scr, bf_scr, kb_scr, bb_scr, if_scr, ib_scr, *, tl):
    c = pl.program_id(2)

    @pl.when(c == 0)
    def _():
        sf_scr[...] = jnp.zeros_like(sf_scr)
        sb_scr[...] = jnp.zeros_like(sb_scr)

    ch = GLA_CHUNK
    nch = tl // ch
    row = lax.broadcasted_iota(jnp.int32, (ch, ch), 0)
    col = lax.broadcasted_iota(jnp.int32, (ch, ch), 1)
    lo = col <= row
    up = col >= row
    t_lo = jnp.where(lo, 1.0, 0.0).astype(BF16)
    t_up = jnp.where(up, 1.0, 0.0).astype(BF16)

    fwd, worst_f = _gla_direction(qf_ref, gf_ref, vf_ref, sf_scr, t_lo, ch - 1, list(range(nch)))
    bwd, worst_b = _gla_direction(qb_ref, gb_ref, vb_ref, sb_scr, t_up, 0, list(range(nch))[::-1])
    safe = jnp.minimum(worst_f, worst_b) >= -SAFE_LOG_DECAY
    work = [(c, of_ref, lo) for c in fwd] + [(c, ob_ref, up) for c in bwd]

    def inter(c):
        return lax.dot_general(c["qe"], c["prior"], _NT, preferred_element_type=F32)

    @pl.when(safe)
    def _():
        scores = []
        for c, _, _ in work:
            ke = (c["k"] * jnp.exp(-c["b"])).astype(BF16)
            scores.append(lax.dot_general(c["qe"], ke, _NT, preferred_element_type=F32))
        inters = [inter(c) for c, _, _ in work]
        probs = [jnp.where(mask, sc, 0.0).astype(BF16) for sc, (_, _, mask) in zip(scores, work)]
        for p, o_in, (c, o_ref, _) in zip(probs, inters, work):
            o_ref[c["rows"], :] = (o_in + jnp.dot(p, c["vb"], preferred_element_type=F32)).astype(o_ref.dtype)

    @pl.when(jnp.logical_not(safe))
    def _():
        for chunks, k_scr, b_scr, i_scr in ((fwd, kf_scr, bf_scr, if_scr), (bwd, kb_scr, bb_scr, ib_scr)):
            for c in chunks:
                k_scr[c["rows"], :] = c["k"]
                b_scr[c["rows"], :] = c["b"]
                i_scr[c["rows"], :] = inter(c)
        _gla_pairwise(qf_ref, vf_ref, kf_scr, bf_scr, if_scr, of_ref, nch, True)
        _gla_pairwise(qb_ref, vb_ref, kb_scr, bb_scr, ib_scr, ob_ref, nch, False)


def _gla(proj, *, batch, seq, heads, group):
    n = proj.shape[0]
    tl = _pick(seq, (512, 256, 128, 64))
    nt = seq // tl
    hb = group // HEAD_DIM
    assert hb == heads

    def fwd(grp):
        return pl.BlockSpec((tl, HEAD_DIM), lambda b, h, c: (b * nt + c, grp * hb + h))

    def bwd(grp):
        return pl.BlockSpec((tl, HEAD_DIM), lambda b, h, c: (b * nt + nt - 1 - c, grp * hb + h))

    out = jax.ShapeDtypeStruct((n, heads * HEAD_DIM), BF16)
    return pl.pallas_call(
        functools.partial(_gla_body, tl=tl),
        grid=(batch, heads, nt),
        in_specs=[fwd(1), fwd(2), fwd(4), bwd(1), bwd(3), bwd(4)],
        out_specs=[pl.BlockSpec((tl, HEAD_DIM), lambda b, h, c: (b * nt + c, h)),
                   pl.BlockSpec((tl, HEAD_DIM), lambda b, h, c: (b * nt + nt - 1 - c, h))],
        out_shape=[out, out],
        scratch_shapes=[pltpu.VMEM((HEAD_DIM, HEAD_DIM), F32)] * 2 + [pltpu.VMEM((tl, HEAD_DIM), F32)] * 6,
        compiler_params=_params(("parallel", "parallel", "arbitrary")),
        name="gla_scan",
    )(proj, proj, proj, proj, proj, proj)


def _conv_ln_rows(xp_scr, y_scr, u_scr, w_ref, cb_ref, lw_ref, lb_ref, slot, base, *, taps, sub):
    nb = xp_scr.shape[0]
    lane = HEAD_DIM
    ch = 64
    first = CONV_HALO - taps // 2
    nwin = (first + taps - 1) // 8 + 1
    for k in range(nb):
        for c0 in range(0, sub, ch):
            wins = [xp_scr[k, pl.ds(base + (c0 + 8 * a), ch + 8), :] for a in range(nwin)]
            acc = jnp.zeros((ch, lane), F32) + cb_ref[k]
            for r in range(8):
                z = None
                for a in range(nwin):
                    j = 8 * a + r - first
                    if 0 <= j < taps:
                        term = w_ref[k, pl.ds(j, 1), :] * wins[a][r:r + ch, :]
                        z = term if z is None else z + term
                if z is not None:
                    acc = acc + z
            y_scr[k, c0:c0 + ch, :] = acc
    inv_c = 1.0 / (nb * lane)
    nr = 32
    for r0 in range(0, sub, nr):
        y = y_scr[:, r0:r0 + nr, :]
        mu = jnp.sum(jnp.sum(y, axis=0), axis=-1, keepdims=True) * inv_c
        yc = y - mu
        var = jnp.sum(jnp.sum(yc * yc, axis=0), axis=-1, keepdims=True) * inv_c
        yn = yc * lax.rsqrt(var + NORM_EPS) * lw_ref[...] + lb_ref[...]
        yn = (yn * _sigmoid(yn)).astype(BF16)
        for k in range(nb):
            u_scr[slot, r0:r0 + nr, k * lane:(k + 1) * lane] = yn[k]


def _mix_rows(x_ref, of_ref, ob_ref, og_ref, hn_ref, wu_ref, wo_ref, o_ref, u_scr, slot, base, *, heads, sub):
    rows = pl.ds(base, sub)
    o = of_ref[rows, :].astype(F32) + ob_ref[rows, :].astype(F32)
    parts = [_rms(o[:, h * HEAD_DIM:(h + 1) * HEAD_DIM], hn_ref[...]) for h in range(heads)]
    on = (jnp.concatenate(parts, axis=-1) * og_ref[rows, :]).astype(BF16)
    y = x_ref[rows, :] + jnp.dot(u_scr[slot], wu_ref[...], preferred_element_type=F32)
    o_ref[rows, :] = y + jnp.dot(on, wo_ref[...], preferred_element_type=F32)


def _mix_out_body(x_ref, c_ref, p_ref, n_ref, of_ref, ob_ref, og_ref, w_ref, cb_ref, lw_ref, lb_ref, hn_ref,
                  wu_ref, wo_ref, o_ref, xp_scr, y_scr, u_scr, *, tm, taps, heads, sub):
    i = pl.program_id(1)
    ni = pl.num_programs(1)
    lane = HEAD_DIM
    for k in range(xp_scr.shape[0]):
        cols = slice(k * lane, (k + 1) * lane)
        xp_scr[k, pl.ds(CONV_HALO, tm), :] = c_ref[:, cols]
        xp_scr[k, pl.ds(0, CONV_HALO), :] = jnp.where(i > 0, p_ref[:, cols], 0.0)
        xp_scr[k, pl.ds(CONV_HALO + tm, CONV_HALO), :] = jnp.where(i < ni - 1, n_ref[:, cols], 0.0)

    conv = functools.partial(_conv_ln_rows, xp_scr, y_scr, u_scr, w_ref, cb_ref, lw_ref, lb_ref, taps=taps, sub=sub)
    mix = functools.partial(_mix_rows, x_ref, of_ref, ob_ref, og_ref, hn_ref, wu_ref, wo_ref, o_ref, u_scr,
                            heads=heads, sub=sub)
    nsub = tm // sub
    conv(0, 0)

    def step(n, carry):
        conv((n + 1) % 2, pl.multiple_of((n + 1) * sub, sub))
        mix(n % 2, pl.multiple_of(n * sub, sub))
        return carry

    lax.fori_loop(0, nsub - 1, step, 0)
    mix((nsub - 1) % 2, (nsub - 1) * sub)


def _mix_out(x, proj, o_f, o_b, conv_w, conv_b, ln_w, ln_b, hn, w_u, w_o, *, batch, seq, heads, group):
    n, d = x.shape
    taps, cw = conv_w.shape
    hw = o_f.shape[1]
    assert hw == group and cw == group and taps // 2 <= CONV_HALO and n == batch * seq and cw % HEAD_DIM == 0
    nb = cw // HEAD_DIM
    tm = _pick(seq, (512, 256, 128))
    sub = min(tm, 128)
    nt = seq // tm
    hb = tm // CONV_HALO
    nhb = n // CONV_HALO
    blocked = lambda t: t.reshape(t.shape[0], nb, HEAD_DIM).transpose(1, 0, 2)
    whole = lambda shape, **kw: pl.BlockSpec(shape, lambda b, i: (0,) * len(shape), **kw)
    tile = lambda width, col=0: pl.BlockSpec((tm, width), lambda b, i: (b * nt + i, col))
    once = pl.Buffered(1)
    return pl.pallas_call(
        functools.partial(_mix_out_body, tm=tm, taps=taps, heads=heads, sub=sub),
        grid=(batch, nt),
        in_specs=[
            tile(d),
            tile(cw),
            pl.BlockSpec((CONV_HALO, cw), lambda b, i: (jnp.maximum((b * nt + i) * hb - 1, 0), 0)),
            pl.BlockSpec((CONV_HALO, cw), lambda b, i: (jnp.minimum((b * nt + i + 1) * hb, nhb - 1), 0)),
            tile(hw),
            tile(hw),
            tile(hw, 5),
            whole((nb, taps, HEAD_DIM)),
            whole((nb, 1, HEAD_DIM)),
            whole((nb, 1, HEAD_DIM)),
            whole((nb, 1, HEAD_DIM)),
            whole((1, HEAD_DIM)),
            whole((cw, d), pipeline_mode=once),
            whole((hw, d), pipeline_mode=once),
        ],
        out_specs=tile(d),
        out_shape=jax.ShapeDtypeStruct((n, d), F32),
        scratch_shapes=[pltpu.VMEM((nb, tm + 2 * CONV_HALO, HEAD_DIM), F32), pltpu.VMEM((nb, sub, HEAD_DIM), F32),
                        pltpu.VMEM((2, sub, cw), BF16)],
        compiler_params=_params(("parallel", "arbitrary")),
        name="mix_out",
    )(x, proj, proj, proj, o_f, o_b, proj, blocked(conv_w), blocked(conv_b), blocked(ln_w), blocked(ln_b), hn,
      w_u, w_o)


def _encoder(x, p):
    batch, seq, d = x.shape
    x = x.reshape(batch * seq, d)
    depth = p["w_in"].shape[0]
    for l in range(depth):
        cw = p["conv_w"].shape[-1]
        hw = p["lb_fwd"].shape[-1]
        heads = hw // HEAD_DIM
        assert cw == hw and p["hg_norm"].shape[-1] == HEAD_DIM
        row = lambda t: t[l].reshape(1, -1)
        x = _ffn(x, row(p["ffn1_norm"]), p["ffn1_w1"][l], p["ffn1_w3"][l], p["ffn1_w2"][l],
                 row(p["ffn1_norm"]), final=False)
        proj = _proj(x, row(p["mix_norm"]), p["w_in"][l], p["lb_fwd"], p["lb_bwd"], layer=l)
        o_f, o_b = _gla(proj, batch=batch, seq=seq, heads=heads, group=hw)
        x = _mix_out(x, proj, o_f, o_b, p["conv_w"][l], row(p["conv_b"]), row(p["conv_ln_w"]), row(p["conv_ln_b"]),
                     row(p["hg_norm"]), p["w_out"][l][:cw], p["w_out"][l][cw:],
                     batch=batch, seq=seq, heads=heads, group=hw)
        last = l == depth - 1
        fn = p["final_norm"].reshape(1, -1)
        x = _ffn(x, row(p["ffn2_norm"]), p["ffn2_w1"][l], p["ffn2_w3"][l], p["ffn2_w2"][l], fn, final=last)
    if depth == 0:
        raise NotImplementedError("depth 0")
    return x.reshape(batch, seq, d)


_MATMUL_WEIGHTS = ("ffn1_w1", "ffn1_w3", "ffn1_w2", "w_in", "w_out", "ffn2_w1", "ffn2_w3", "ffn2_w2")


def kernel(x_prompt, x_sample, ffn1_norm, ffn1_w1, ffn1_w3, ffn1_w2, mix_norm, w_in, conv_w, conv_b, conv_ln_w,
           conv_ln_b, lb_fwd, lb_bwd, hg_norm, w_out, ffn2_norm, ffn2_w1, ffn2_w3, ffn2_w2, final_norm):
    p = dict(ffn1_norm=ffn1_norm, ffn1_w1=ffn1_w1, ffn1_w3=ffn1_w3, ffn1_w2=ffn1_w2, mix_norm=mix_norm, w_in=w_in,
             conv_w=conv_w, conv_b=conv_b, conv_ln_w=conv_ln_w, conv_ln_b=conv_ln_b, lb_fwd=lb_fwd, lb_bwd=lb_bwd,
             hg_norm=hg_norm, w_out=w_out, ffn2_norm=ffn2_norm, ffn2_w1=ffn2_w1, ffn2_w3=ffn2_w3, ffn2_w2=ffn2_w2,
             final_norm=final_norm)
    for name in _MATMUL_WEIGHTS:
        p[name] = p[name].astype(BF16)
    return _encoder(x_prompt, p), _encoder(x_sample, p)
```

```python
import functools

import jax
import jax.numpy as jnp
from jax import lax
from jax.experimental import pallas as pl
from jax.experimental.pallas import tpu as pltpu

NORM_EPS = 1e-6
HEAD_DIM = 128
GLA_CHUNK = 64
CONV_HALO = 16
SAFE_LOG_DECAY = 60.0
V7X_VMEM_LIMIT = 56 * 1024 * 1024
V7X_VMEM_LIMIT_FFN = 60 * 1024 * 1024

F32 = jnp.float32
BF16 = jnp.bfloat16


def _sigmoid(x):
    return 1.0 / (1.0 + jnp.exp(-x))


def _rms(x, w):
    return x * lax.rsqrt(jnp.mean(x * x, axis=-1, keepdims=True) + NORM_EPS) * w


def _params(sem, vmem=V7X_VMEM_LIMIT):
    return pltpu.CompilerParams(dimension_semantics=sem, vmem_limit_bytes=vmem)


def _pick(n, prefs):
    for p in prefs:
        if n % p == 0:
            return p
    return n


def _ffn_body(x_ref, nw_ref, w1_ref, w3_ref, w2_ref, fn_ref, o_ref, h_scr, *, nj, final, sub, sub0):
    j = pl.program_id(1)
    tm = h_scr.shape[0]

    def swiglu(h):
        a = jnp.dot(h, w1_ref[...], preferred_element_type=F32)
        b = jnp.dot(h, w3_ref[...], preferred_element_type=F32)
        p = (a * _sigmoid(a) * b).astype(BF16)
        return jnp.dot(p, w2_ref[...], preferred_element_type=F32)

    @pl.when(j == 0)
    def _():
        for r0 in range(0, tm, sub0):
            rows = slice(r0, r0 + sub0)
            h = _rms(x_ref[rows, :], nw_ref[...]).astype(BF16)
            h_scr[rows, :] = h
            o_ref[rows, :] = swiglu(h)

    @pl.when(j > 0)
    def _():
        for r0 in range(0, tm, sub):
            rows = slice(r0, r0 + sub)
            o_ref[rows, :] += swiglu(h_scr[rows, :])

    @pl.when(j == nj - 1)
    def _():
        y = x_ref[...] + 0.5 * o_ref[...]
        if final:
            y = _rms(y, fn_ref[...])
        o_ref[...] = y


def _ffn(x, nw, w1, w3, w2, fn, *, final):
    n, d = x.shape
    f = w1.shape[1]
    tm = _pick(n, (1024, 512, 256, 128, 64, 32, 16, 8))
    tf = _pick(f, (512, 256, 128))
    nj = f // tf
    return pl.pallas_call(
        functools.partial(_ffn_body, nj=nj, final=final, sub=min(tm, 512), sub0=min(tm, 256)),
        grid=(n // tm, nj),
        in_specs=[
            pl.BlockSpec((tm, d), lambda i, j: (i, 0)),
            pl.BlockSpec((1, d), lambda i, j: (0, 0)),
            pl.BlockSpec((d, tf), lambda i, j: (0, j)),
            pl.BlockSpec((d, tf), lambda i, j: (0, j)),
            pl.BlockSpec((tf, d), lambda i, j: (j, 0)),
            pl.BlockSpec((1, d), lambda i, j: (0, 0)),
        ],
        out_specs=pl.BlockSpec((tm, d), lambda i, j: (i, 0)),
        out_shape=jax.ShapeDtypeStruct((n, d), F32),
        scratch_shapes=[pltpu.VMEM((tm, d), BF16)],
        compiler_params=_params(("parallel", "arbitrary"), V7X_VMEM_LIMIT_FFN),
        name="ffn_final" if final else "ffn",
    )(x, nw, w1, w3, w2, fn)


def _lower_bound(lb_ref, layer):
    lb = lb_ref[...]
    e = jnp.exp(lb - jnp.max(lb, axis=0, keepdims=True))
    return jnp.sum(e[: layer + 1], axis=0, keepdims=True) / jnp.sum(e, axis=0, keepdims=True)


def _proj_body(x_ref, nw_ref, w_ref, lbf_ref, lbb_ref, o_ref, *, layer, g):
    h = _rms(x_ref[...], nw_ref[...]).astype(BF16)

    def group(j):
        return jnp.dot(h, w_ref[:, j * g:(j + 1) * g], preferred_element_type=F32)

    def put(j, val):
        for h in range(g // HEAD_DIM):
            o_ref[j * (g // HEAD_DIM) + h] = val[:, h * HEAD_DIM:(h + 1) * HEAD_DIM]

    put(0, group(0) * _sigmoid(group(1)))
    q = group(2)
    put(1, q * _sigmoid(q))
    for j_out, j_in, lb_ref in ((2, 3, lbf_ref), (3, 4, lbb_ref)):
        lb = _lower_bound(lb_ref, layer)
        put(j_out, jnp.log(lb + (1.0 - lb) * _sigmoid(group(j_in))))
    put(4, group(5))
    og = group(6)
    put(5, og * _sigmoid(og))


def _proj(x, nw, w_in, lbf, lbb, *, layer):
    n, d = x.shape
    g = lbf.shape[1]
    assert w_in.shape[1] == 7 * g
    tm = _pick(n, (256, 128, 64, 32, 16, 8))
    nl = lbf.shape[0]
    once = pl.Buffered(1)
    return pl.pallas_call(
        functools.partial(_proj_body, layer=layer, g=g),
        grid=(n // tm,),
        in_specs=[
            pl.BlockSpec((tm, d), lambda i: (i, 0)),
            pl.BlockSpec((1, d), lambda i: (0, 0)),
            pl.BlockSpec((d, 7 * g), lambda i: (0, 0), pipeline_mode=once),
            pl.BlockSpec((nl, g), lambda i: (0, 0)),
            pl.BlockSpec((nl, g), lambda i: (0, 0)),
        ],
        out_specs=pl.BlockSpec((6 * g // HEAD_DIM, tm, HEAD_DIM), lambda i: (0, i, 0)),
        out_shape=jax.ShapeDtypeStruct((6 * g // HEAD_DIM, n, HEAD_DIM), F32),
        compiler_params=_params(("parallel",)),
        name="in_proj",
    )(x, nw, w_in, lbf, lbb)


def _split3(g):
    def head(x):
        bits = pltpu.bitcast(x, jnp.uint32) & jnp.uint32(0xFFFF0000)
        return pltpu.bitcast(bits, F32)
    g1 = head(g)
    r1 = g - g1
    g2 = head(r1)
    g3 = r1 - g2
    return g1.astype(BF16), g2.astype(BF16), g3.astype(BF16)


_NT = (((1,), (1,)), ((), ()))
_TN = (((0,), (0,)), ((), ()))


def _gla_direction(q_ref, g_ref, v_ref, s_scr, tcum, last, order):
    ch = GLA_CHUNK
    g = g_ref[...]
    g1, g2, g3 = _split3(g)
    pieces = [p[i * ch:(i + 1) * ch] for i in order for p in (g1, g2, g3)]
    cs = jnp.dot(tcum, jnp.concatenate(pieces, axis=1), preferred_element_type=F32)
    chunks = []
    worst = None
    for n, i in enumerate(order):
        rows = slice(i * ch, (i + 1) * ch)
        c0 = 3 * HEAD_DIM * n
        b = cs[:, c0:c0 + HEAD_DIM] + cs[:, c0 + HEAD_DIM:c0 + 2 * HEAD_DIM] + cs[:, c0 + 2 * HEAD_DIM:c0 + 3 * HEAD_DIM]
        k = 1.0 - jnp.exp(g[rows])
        bl = b[last:last + 1, :]
        worst = bl if worst is None else jnp.minimum(worst, bl)
        chunks.append(dict(rows=rows, b=b, k=k, bl=bl, qe=(q_ref[rows, :] * jnp.exp(b)).astype(BF16),
                           vb=v_ref[rows, :].astype(BF16)))
    st = s_scr[...]
    for c in chunks:
        k2 = (c["k"] * jnp.exp(c["bl"] - c["b"])).astype(BF16)
        c["prior"] = st.astype(BF16)
        st = st * jnp.exp(c["bl"]) + lax.dot_general(c["vb"], k2, _TN, preferred_element_type=F32)
    s_scr[...] = st
    return chunks, jnp.min(worst)


def _gla_pairwise(q_ref, v_ref, k_scr, b_scr, i_scr, o_ref, n_chunks, forward):
    ch = GLA_CHUNK
    t_idx = lax.broadcasted_iota(jnp.int32, (ch, 1), 0)

    def one_chunk(i, carry):
        rows = pl.ds(pl.multiple_of(i * ch, ch), ch)
        q = q_ref[rows, :]
        v = v_ref[rows, :]
        k = k_scr[rows, :]
        b = b_scr[rows, :]
        acc = jnp.zeros((ch, HEAD_DIM), F32)
        for s in range(ch):
            e = jnp.exp(jnp.minimum(b - b[s:s + 1, :], 0.0))
            w = jnp.sum(q * k[s:s + 1, :] * e, axis=-1, keepdims=True)
            seen = (t_idx >= s) if forward else (t_idx <= s)
            acc = acc + jnp.where(seen, w, 0.0) * v[s:s + 1, :]
        o_ref[rows, :] = (i_scr[rows, :] + acc).astype(o_ref.dtype)
        return carry

    lax.fori_loop(0, n_chunks, one_chunk, 0)


def _gla_body(qf_ref, gf_ref, vf_ref, qb_ref, gb_ref, vb_ref, of_ref, ob_ref,
              sf_scr, sb_scr, kf_scr, bf_scr, kb_scr, bb_scr, if_scr, ib_scr, *, tl):
    c = pl.program_id(2)

    @pl.when(c == 0)
    def _():
        sf_scr[...] = jnp.zeros_like(sf_scr)
        sb_scr[...] = jnp.zeros_like(sb_scr)

    ch = GLA_CHUNK
    nch = tl // ch
    row = lax.broadcasted_iota(jnp.int32, (ch, ch), 0)
    col = lax.broadcasted_iota(jnp.int32, (ch, ch), 1)
    lo = col <= row
    up = col >= row
    t_lo = jnp.where(lo, 1.0, 0.0).astype(BF16)
    t_up = jnp.where(up, 1.0, 0.0).astype(BF16)

    fwd, worst_f = _gla_direction(qf_ref, gf_ref, vf_ref, sf_scr, t_lo, ch - 1, list(range(nch)))
    bwd, worst_b = _gla_direction(qb_ref, gb_ref, vb_ref, sb_scr, t_up, 0, list(range(nch))[::-1])
    safe = jnp.minimum(worst_f, worst_b) >= -SAFE_LOG_DECAY
    work = [(c, of_ref, lo) for c in fwd] + [(c, ob_ref, up) for c in bwd]

    def inter(c):
        return lax.dot_general(c["qe"], c["prior"], _NT, preferred_element_type=F32)

    @pl.when(safe)
    def _():
        scores = []
        for c, _, _ in work:
            ke = (c["k"] * jnp.exp(-c["b"])).astype(BF16)
            scores.append(lax.dot_general(c["qe"], ke, _NT, preferred_element_type=F32))
        inters = [inter(c) for c, _, _ in work]
        probs = [jnp.where(mask, sc, 0.0).astype(BF16) for sc, (_, _, mask) in zip(scores, work)]
        for p, o_in, (c, o_ref, _) in zip(probs, inters, work):
            o_ref[c["rows"], :] = (o_in + jnp.dot(p, c["vb"], preferred_element_type=F32)).astype(o_ref.dtype)

    @pl.when(jnp.logical_not(safe))
    def _():
        for chunks, k_scr, b_scr, i_scr in ((fwd, kf_scr, bf_scr, if_scr), (bwd, kb_scr, bb_scr, ib_scr)):
            for c in chunks:
                k_scr[c["rows"], :] = c["k"]
                b_scr[c["rows"], :] = c["b"]
                i_scr[c["rows"], :] = inter(c)
        _gla_pairwise(qf_ref, vf_ref, kf_scr, bf_scr, if_scr, of_ref, nch, True)
        _gla_pairwise(qb_ref, vb_ref, kb_scr, bb_scr, ib_scr, ob_ref, nch, False)


def _gla(proj, *, batch, seq, heads, group):
    n = proj.shape[1]
    tl = _pick(seq, (512, 256, 128, 64))
    nt = seq // tl
    hb = group // HEAD_DIM
    assert hb == heads

    def fwd(grp):
        return pl.BlockSpec((None, tl, HEAD_DIM), lambda b, h, c: (grp * hb + h, b * nt + c, 0))

    def bwd(grp):
        return pl.BlockSpec((None, tl, HEAD_DIM), lambda b, h, c: (grp * hb + h, b * nt + nt - 1 - c, 0))

    out = jax.ShapeDtypeStruct((heads, n, HEAD_DIM), BF16)
    return pl.pallas_call(
        functools.partial(_gla_body, tl=tl),
        grid=(batch, heads, nt),
        in_specs=[fwd(1), fwd(2), fwd(4), bwd(1), bwd(3), bwd(4)],
        out_specs=[pl.BlockSpec((None, tl, HEAD_DIM), lambda b, h, c: (h, b * nt + c, 0)),
                   pl.BlockSpec((None, tl, HEAD_DIM), lambda b, h, c: (h, b * nt + nt - 1 - c, 0))],
        out_shape=[out, out],
        scratch_shapes=[pltpu.VMEM((HEAD_DIM, HEAD_DIM), F32)] * 2 + [pltpu.VMEM((tl, HEAD_DIM), F32)] * 6,
        compiler_params=_params(("parallel", "parallel", "arbitrary")),
        name="gla_scan",
    )(proj, proj, proj, proj, proj, proj)


def _conv_ln_rows(xp_scr, y_scr, u_scr, w_ref, cb_ref, lw_ref, lb_ref, slot, base, *, taps, sub):
    nb = xp_scr.shape[0]
    lane = HEAD_DIM
    ch = min(sub, 128)
    first = CONV_HALO - taps // 2
    nwin = (first + taps - 1) // 8 + 1
    for k in range(nb):
        for c0 in range(0, sub, ch):
            wins = [xp_scr[k, pl.ds(base + (c0 + 8 * a), ch + 8), :] for a in range(nwin)]
            acc = jnp.zeros((ch, lane), F32) + cb_ref[k]
            for r in range(8):
                z = None
                for a in range(nwin):
                    j = 8 * a + r - first
                    if 0 <= j < taps:
                        term = w_ref[k, pl.ds(j, 1), :] * wins[a][r:r + ch, :]
                        z = term if z is None else z + term
                if z is not None:
                    acc = acc + z
            y_scr[k, c0:c0 + ch, :] = acc
    inv_c = 1.0 / (nb * lane)
    nr = 32
    for r0 in range(0, sub, nr):
        y = y_scr[:, r0:r0 + nr, :]
        mu = jnp.sum(jnp.sum(y, axis=0), axis=-1, keepdims=True) * inv_c
        yc = y - mu
        var = jnp.sum(jnp.sum(yc * yc, axis=0), axis=-1, keepdims=True) * inv_c
        yn = yc * lax.rsqrt(var + NORM_EPS) * lw_ref[...] + lb_ref[...]
        yn = (yn * _sigmoid(yn)).astype(BF16)
        for k in range(nb):
            u_scr[slot, r0:r0 + nr, k * lane:(k + 1) * lane] = yn[k]


def _mix_rows(x_ref, of_ref, ob_ref, og_ref, hn_ref, w_ref, o_ref, u_scr, slot, base, *, heads, sub):
    rows = pl.ds(base, sub)
    parts = []
    for h in range(heads):
        o = of_ref[h, rows, :].astype(F32) + ob_ref[h, rows, :].astype(F32)
        parts.append((_rms(o, hn_ref[...]) * og_ref[h, rows, :]).astype(BF16))
    on = jnp.concatenate(parts, axis=-1)
    mixed = jnp.concatenate([u_scr[slot], on], axis=-1)
    o_ref[rows, :] = x_ref[rows, :] + jnp.dot(mixed, w_ref[...], preferred_element_type=F32)


def _mix_out_body(x_ref, c_ref, p_ref, n_ref, of_ref, ob_ref, og_ref, w_ref, cb_ref, lw_ref, lb_ref, hn_ref,
                  wout_ref, o_ref, xp_scr, y_scr, u_scr, *, tm, taps, heads, sub):
    i = pl.program_id(1)
    ni = pl.num_programs(1)
    xp_scr[:, pl.ds(CONV_HALO, tm), :] = c_ref[...]
    xp_scr[:, pl.ds(0, CONV_HALO), :] = jnp.where(i > 0, p_ref[...], 0.0)
    xp_scr[:, pl.ds(CONV_HALO + tm, CONV_HALO), :] = jnp.where(i < ni - 1, n_ref[...], 0.0)

    conv = functools.partial(_conv_ln_rows, xp_scr, y_scr, u_scr, w_ref, cb_ref, lw_ref, lb_ref, taps=taps, sub=sub)
    mix = functools.partial(_mix_rows, x_ref, of_ref, ob_ref, og_ref, hn_ref, wout_ref, o_ref, u_scr,
                            heads=heads, sub=sub)
    nsub = tm // sub
    conv(0, 0)

    def step(n, carry):
        conv((n + 1) % 2, pl.multiple_of((n + 1) * sub, sub))
        mix(n % 2, pl.multiple_of(n * sub, sub))
        return carry

    lax.fori_loop(0, nsub - 1, step, 0)
    mix((nsub - 1) % 2, (nsub - 1) * sub)


def _mix_out(x, proj, o_f, o_b, conv_w, conv_b, ln_w, ln_b, hn, w_out, *, batch, seq, heads, group):
    n, d = x.shape
    taps, cw = conv_w.shape
    hw = o_f.shape[0] * HEAD_DIM
    assert hw == group and cw == group and taps // 2 <= CONV_HALO and n == batch * seq and cw % HEAD_DIM == 0
    assert w_out.shape == (cw + hw, d)
    nb = cw // HEAD_DIM
    tm = _pick(seq, (512, 256, 128))
    sub = min(tm, 128)
    nt = seq // tm
    hb = tm // CONV_HALO
    nhb = n // CONV_HALO
    blocked = lambda t: t.reshape(t.shape[0], nb, HEAD_DIM).transpose(1, 0, 2)
    whole = lambda shape, **kw: pl.BlockSpec(shape, lambda b, i: (0,) * len(shape), **kw)
    tile = lambda width: pl.BlockSpec((tm, width), lambda b, i: (b * nt + i, 0))
    heads_tile = lambda grp=0: pl.BlockSpec((nb, tm, HEAD_DIM), lambda b, i: (grp, b * nt + i, 0))
    once = pl.Buffered(1)
    return pl.pallas_call(
        functools.partial(_mix_out_body, tm=tm, taps=taps, heads=heads, sub=sub),
        grid=(batch, nt),
        in_specs=[
            tile(d),
            heads_tile(0),
            pl.BlockSpec((nb, CONV_HALO, HEAD_DIM), lambda b, i: (0, jnp.maximum((b * nt + i) * hb - 1, 0), 0)),
            pl.BlockSpec((nb, CONV_HALO, HEAD_DIM), lambda b, i: (0, jnp.minimum((b * nt + i + 1) * hb, nhb - 1), 0)),
            heads_tile(),
            heads_tile(),
            heads_tile(5),
            whole((nb, taps, HEAD_DIM)),
            whole((nb, 1, HEAD_DIM)),
            whole((nb, 1, HEAD_DIM)),
            whole((nb, 1, HEAD_DIM)),
            whole((1, HEAD_DIM)),
            whole((cw + hw, d), pipeline_mode=once),
        ],
        out_specs=tile(d),
        out_shape=jax.ShapeDtypeStruct((n, d), F32),
        scratch_shapes=[pltpu.VMEM((nb, tm + 2 * CONV_HALO, HEAD_DIM), F32), pltpu.VMEM((nb, sub, HEAD_DIM), F32),
                        pltpu.VMEM((2, sub, cw), BF16)],
        compiler_params=_params(("parallel", "arbitrary")),
        name="mix_out",
    )(x, proj, proj, proj, o_f, o_b, proj, blocked(conv_w), blocked(conv_b), blocked(ln_w), blocked(ln_b), hn,
      w_out)


def _encoder(x, p):
    batch, seq, d = x.shape
    x = x.reshape(batch * seq, d)
    depth = p["w_in"].shape[0]
    for l in range(depth):
        cw = p["conv_w"].shape[-1]
        hw = p["lb_fwd"].shape[-1]
        heads = hw // HEAD_DIM
        assert cw == hw and p["hg_norm"].shape[-1] == HEAD_DIM
        row = lambda t: t[l].reshape(1, -1)
        x = _ffn(x, row(p["ffn1_norm"]), p["ffn1_w1"][l], p["ffn1_w3"][l], p["ffn1_w2"][l],
                 row(p["ffn1_norm"]), final=False)
        proj = _proj(x, row(p["mix_norm"]), p["w_in"][l], p["lb_fwd"], p["lb_bwd"], layer=l)
        o_f, o_b = _gla(proj, batch=batch, seq=seq, heads=heads, group=hw)
        x = _mix_out(x, proj, o_f, o_b, p["conv_w"][l], row(p["conv_b"]), row(p["conv_ln_w"]), row(p["conv_ln_b"]),
                     row(p["hg_norm"]), p["w_out"][l],
                     batch=batch, seq=seq, heads=heads, group=hw)
        last = l == depth - 1
        fn = p["final_norm"].reshape(1, -1)
        x = _ffn(x, row(p["ffn2_norm"]), p["ffn2_w1"][l], p["ffn2_w3"][l], p["ffn2_w2"][l], fn, final=last)
    if depth == 0:
        raise NotImplementedError("depth 0")
    return x.reshape(batch, seq, d)


_MATMUL_WEIGHTS = ("ffn1_w1", "ffn1_w3", "ffn1_w2", "w_in", "w_out", "ffn2_w1", "ffn2_w3", "ffn2_w2")


def kernel(x_prompt, x_sample, ffn1_norm, ffn1_w1, ffn1_w3, ffn1_w2, mix_norm, w_in, conv_w, conv_b, conv_ln_w,
           conv_ln_b, lb_fwd, lb_bwd, hg_norm, w_out, ffn2_norm, ffn2_w1, ffn2_w3, ffn2_w2, final_norm):
    p = dict(ffn1_norm=ffn1_norm, ffn1_w1=ffn1_w1, ffn1_w3=ffn1_w3, ffn1_w2=ffn1_w2, mix_norm=mix_norm, w_in=w_in,
             conv_w=conv_w, conv_b=conv_b, conv_ln_w=conv_ln_w, conv_ln_b=conv_ln_b, lb_fwd=lb_fwd, lb_bwd=lb_bwd,
             hg_norm=hg_norm, w_out=w_out, ffn2_norm=ffn2_norm, ffn2_w1=ffn2_w1, ffn2_w3=ffn2_w3, ffn2_w2=ffn2_w2,
             final_norm=final_norm)
    for name in _MATMUL_WEIGHTS:
        p[name] = p[name].astype(BF16)
    return _encoder(x_prompt, p), _encoder(x_sample, p)
```

```python
import functools

import jax
import jax.numpy as jnp
from jax import lax
from jax.experimental import pallas as pl
from jax.experimental.pallas import tpu as pltpu

NORM_EPS = 1e-6
HEAD_DIM = 128
GLA_CHUNK = 64
CONV_HALO = 16
SAFE_LOG_DECAY = 60.0
V7X_VMEM_BYTES = 64 * 1024 * 1024
V7X_VMEM_LIMIT = V7X_VMEM_BYTES - 8 * 1024 * 1024
V7X_VMEM_LIMIT_FFN = V7X_VMEM_BYTES - 4 * 1024 * 1024

F32 = jnp.float32
BF16 = jnp.bfloat16


def _sigmoid(x):
    return 1.0 / (1.0 + jnp.exp(-x))


def _rms(x, w):
    return x * lax.rsqrt(jnp.mean(x * x, axis=-1, keepdims=True) + NORM_EPS) * w


def _params(sem, vmem=V7X_VMEM_LIMIT):
    return pltpu.CompilerParams(dimension_semantics=sem, vmem_limit_bytes=vmem)


def _pick(n, prefs):
    for p in prefs:
        if n % p == 0:
            return p
    return n


def _ffn_body(x_ref, nw_ref, w1_ref, w3_ref, w2_ref, fn_ref, o_ref, h_scr, *, nj, final, sub, sub0):
    j = pl.program_id(1)
    tm = h_scr.shape[0]

    def swiglu(h):
        a = jnp.dot(h, w1_ref[...], preferred_element_type=F32)
        b = jnp.dot(h, w3_ref[...], preferred_element_type=F32)
        p = (a * _sigmoid(a) * b).astype(BF16)
        return jnp.dot(p, w2_ref[...], preferred_element_type=F32)

    def ff_block(first, last, step):
        for r0 in range(0, tm, step):
            rows = slice(r0, r0 + step)
            if first:
                h = _rms(x_ref[rows, :], nw_ref[...]).astype(BF16)
                h_scr[rows, :] = h
                acc = swiglu(h)
            else:
                acc = o_ref[rows, :] + swiglu(h_scr[rows, :])
            if last:
                acc = x_ref[rows, :] + 0.5 * acc
                if final:
                    acc = _rms(acc, fn_ref[...])
            o_ref[rows, :] = acc

    if nj == 1:
        ff_block(True, True, sub0)
    else:
        pl.when(j == 0)(lambda: ff_block(True, False, sub0))
        pl.when((j > 0) & (j < nj - 1))(lambda: ff_block(False, False, sub))
        pl.when(j == nj - 1)(lambda: ff_block(False, True, sub0))


def _ffn(x, nw, w1, w3, w2, fn, *, final):
    n, d = x.shape
    f = w1.shape[1]
    tm = _pick(n, (1024, 512, 256, 128, 64, 32, 16, 8))
    tf = _pick(f, (512, 256, 128))
    nj = f // tf
    return pl.pallas_call(
        functools.partial(_ffn_body, nj=nj, final=final, sub=min(tm, 512), sub0=min(tm, 256)),
        grid=(n // tm, nj),
        in_specs=[
            pl.BlockSpec((tm, d), lambda i, j: (i, 0)),
            pl.BlockSpec((1, d), lambda i, j: (0, 0)),
            pl.BlockSpec((d, tf), lambda i, j: (0, j)),
            pl.BlockSpec((d, tf), lambda i, j: (0, j)),
            pl.BlockSpec((tf, d), lambda i, j: (j, 0)),
            pl.BlockSpec((1, d), lambda i, j: (0, 0)),
        ],
        out_specs=pl.BlockSpec((tm, d), lambda i, j: (i, 0)),
        out_shape=jax.ShapeDtypeStruct((n, d), F32),
        scratch_shapes=[pltpu.VMEM((tm, d), BF16)],
        compiler_params=_params(("parallel", "arbitrary"), V7X_VMEM_LIMIT_FFN),
        name="ffn_final" if final else "ffn",
    )(x, nw, w1, w3, w2, fn)


def _lower_bound(lb_ref, layer):
    lb = lb_ref[...]
    e = jnp.exp(lb - jnp.max(lb, axis=0, keepdims=True))
    return jnp.sum(e[: layer + 1], axis=0, keepdims=True) / jnp.sum(e, axis=0, keepdims=True)


def _proj_body(x_ref, nw_ref, w_ref, lbf_ref, lbb_ref, o_ref, *, layer, g):
    h = _rms(x_ref[...], nw_ref[...]).astype(BF16)

    def group(j):
        return jnp.dot(h, w_ref[:, j * g:(j + 1) * g], preferred_element_type=F32)

    def put(j, val):
        for hd in range(g // HEAD_DIM):
            o_ref[j * (g // HEAD_DIM) + hd] = val[:, hd * HEAD_DIM:(hd + 1) * HEAD_DIM]

    put(0, group(0) * _sigmoid(group(1)))
    q = group(2)
    put(1, q * _sigmoid(q))
    og = group(6)
    put(5, og * _sigmoid(og))
    for j_out, j_in, lb_ref in ((2, 3, lbf_ref), (3, 4, lbb_ref)):
        lb = _lower_bound(lb_ref, layer)
        put(j_out, jnp.log(lb + (1.0 - lb) * _sigmoid(group(j_in))))
    put(4, group(5))


def _proj(x, nw, w_in, lbf, lbb, *, layer):
    n, d = x.shape
    g = lbf.shape[1]
    assert w_in.shape[1] == 7 * g
    tm = _pick(n, (256, 128, 64, 32, 16, 8))
    nl = lbf.shape[0]
    once = pl.Buffered(1)
    return pl.pallas_call(
        functools.partial(_proj_body, layer=layer, g=g),
        grid=(n // tm,),
        in_specs=[
            pl.BlockSpec((tm, d), lambda i: (i, 0)),
            pl.BlockSpec((1, d), lambda i: (0, 0)),
            pl.BlockSpec((d, 7 * g), lambda i: (0, 0), pipeline_mode=once),
            pl.BlockSpec((nl, g), lambda i: (0, 0)),
            pl.BlockSpec((nl, g), lambda i: (0, 0)),
        ],
        out_specs=pl.BlockSpec((6 * g // HEAD_DIM, tm, HEAD_DIM), lambda i: (0, i, 0)),
        out_shape=jax.ShapeDtypeStruct((6 * g // HEAD_DIM, n, HEAD_DIM), F32),
        compiler_params=_params(("parallel",)),
        name="in_proj",
    )(x, nw, w_in, lbf, lbb)


def _split3(g):
    def head(x):
        bits = pltpu.bitcast(x, jnp.uint32) & jnp.uint32(0xFFFF0000)
        return pltpu.bitcast(bits, F32)
    g1 = head(g)
    r1 = g - g1
    g2 = head(r1)
    g3 = r1 - g2
    return g1.astype(BF16), g2.astype(BF16), g3.astype(BF16)


_NT = (((1,), (1,)), ((), ()))
_TN = (((0,), (0,)), ((), ()))


def _gla_direction(q_ref, g_ref, v_ref, s_scr, tcum, last, order):
    ch = GLA_CHUNK
    g = g_ref[...]
    g1, g2, g3 = _split3(g)
    pieces = [p[i * ch:(i + 1) * ch] for i in order for p in (g1, g2, g3)]
    cs = jnp.dot(tcum, jnp.concatenate(pieces, axis=1), preferred_element_type=F32)
    chunks = []
    worst = None
    for n, i in enumerate(order):
        rows = slice(i * ch, (i + 1) * ch)
        c0 = 3 * HEAD_DIM * n
        b = cs[:, c0:c0 + HEAD_DIM] + cs[:, c0 + HEAD_DIM:c0 + 2 * HEAD_DIM] + cs[:, c0 + 2 * HEAD_DIM:c0 + 3 * HEAD_DIM]
        k = 1.0 - jnp.exp(g[rows])
        bl = b[last:last + 1, :]
        worst = bl if worst is None else jnp.minimum(worst, bl)
        chunks.append(dict(rows=rows, b=b, k=k, bl=bl, qe=(q_ref[rows, :] * jnp.exp(b)).astype(BF16),
                           vb=v_ref[rows, :].astype(BF16)))
    st = s_scr[...]
    for c in chunks:
        k2 = (c["k"] * jnp.exp(c["bl"] - c["b"])).astype(BF16)
        c["prior"] = st.astype(BF16)
        st = st * jnp.exp(c["bl"]) + lax.dot_general(c["vb"], k2, _TN, preferred_element_type=F32)
    s_scr[...] = st
    return chunks, jnp.min(worst)


def _gla_pairwise(q_ref, v_ref, k_scr, b_scr, i_scr, o_ref, n_chunks, forward):
    ch = GLA_CHUNK
    t_idx = lax.broadcasted_iota(jnp.int32, (ch, 1), 0)

    def one_chunk(i, carry):
        rows = pl.ds(pl.multiple_of(i * ch, ch), ch)
        q = q_ref[rows, :]
        v = v_ref[rows, :]
        k = k_scr[rows, :]
        b = b_scr[rows, :]
        acc = jnp.zeros((ch, HEAD_DIM), F32)
        for s in range(ch):
            e = jnp.exp(jnp.minimum(b - b[s:s + 1, :], 0.0))
            w = jnp.sum(q * k[s:s + 1, :] * e, axis=-1, keepdims=True)
            seen = (t_idx >= s) if forward else (t_idx <= s)
            acc = acc + jnp.where(seen, w, 0.0) * v[s:s + 1, :]
        o_ref[rows, :] = (i_scr[rows, :] + acc).astype(o_ref.dtype)
        return carry

    lax.fori_loop(0, n_chunks, one_chunk, 0)


def _gla_body(qf_ref, gf_ref, vf_ref, qb_ref, gb_ref, vb_ref, of_ref, ob_ref,
              sf_scr, sb_scr, kf_scr, bf_scr, kb_scr, bb_scr, if_scr, ib_scr, *, tl):
    c = pl.program_id(2)

    @pl.when(c == 0)
    def _():
        sf_scr[...] = jnp.zeros_like(sf_scr)
        sb_scr[...] = jnp.zeros_like(sb_scr)

    ch = GLA_CHUNK
    nch = tl // ch
    heads = qf_ref.shape[0]
    row = lax.broadcasted_iota(jnp.int32, (ch, ch), 0)
    col = lax.broadcasted_iota(jnp.int32, (ch, ch), 1)
    lo = col <= row
    up = col >= row
    t_lo = jnp.where(lo, 1.0, 0.0).astype(BF16)
    t_up = jnp.where(up, 1.0, 0.0).astype(BF16)

    fwd, bwd, work, worst = [], [], [], None
    for h in range(heads):
        f, worst_f = _gla_direction(qf_ref.at[h], gf_ref.at[h], vf_ref.at[h], sf_scr.at[h], t_lo, ch - 1,
                                    list(range(nch)))
        b, worst_b = _gla_direction(qb_ref.at[h], gb_ref.at[h], vb_ref.at[h], sb_scr.at[h], t_up, 0,
                                    list(range(nch))[::-1])
        fwd.append(f)
        bwd.append(b)
        work += [(c, of_ref.at[h], lo) for c in f] + [(c, ob_ref.at[h], up) for c in b]
        least = jnp.minimum(worst_f, worst_b)
        worst = least if worst is None else jnp.minimum(worst, least)
    safe = worst >= -SAFE_LOG_DECAY

    def inter(c):
        return lax.dot_general(c["qe"], c["prior"], _NT, preferred_element_type=F32)

    @pl.when(safe)
    def _():
        scores = []
        for c, _, _ in work:
            ke = (c["k"] * jnp.exp(-c["b"])).astype(BF16)
            scores.append(lax.dot_general(c["qe"], ke, _NT, preferred_element_type=F32))
        inters = [inter(c) for c, _, _ in work]
        probs = [jnp.where(mask, sc, 0.0).astype(BF16) for sc, (_, _, mask) in zip(scores, work)]
        for p, o_in, (c, o_ref, _) in zip(probs, inters, work):
            o_ref[c["rows"], :] = (o_in + jnp.dot(p, c["vb"], preferred_element_type=F32)).astype(o_ref.dtype)

    @pl.when(jnp.logical_not(safe))
    def _():
        for h in range(heads):
            for chunks, k_scr, b_scr, i_scr in ((fwd[h], kf_scr, bf_scr, if_scr), (bwd[h], kb_scr, bb_scr, ib_scr)):
                for c in chunks:
                    k_scr[h, c["rows"], :] = c["k"]
                    b_scr[h, c["rows"], :] = c["b"]
                    i_scr[h, c["rows"], :] = inter(c)
            _gla_pairwise(qf_ref.at[h], vf_ref.at[h], kf_scr.at[h], bf_scr.at[h], if_scr.at[h], of_ref.at[h],
                          nch, True)
            _gla_pairwise(qb_ref.at[h], vb_ref.at[h], kb_scr.at[h], bb_scr.at[h], ib_scr.at[h], ob_ref.at[h],
                          nch, False)


def _gla(proj, *, batch, seq, heads, group):
    n = proj.shape[1]
    tl = _pick(seq, (512, 256, 128, 64))
    nt = seq // tl
    hb = group // HEAD_DIM
    assert hb == heads
    hs = 2 if heads % 2 == 0 else 1

    def fwd(grp):
        return pl.BlockSpec((hs, tl, HEAD_DIM), lambda b, h, c: ((grp * hb) // hs + h, b * nt + c, 0))

    def bwd(grp):
        return pl.BlockSpec((hs, tl, HEAD_DIM), lambda b, h, c: ((grp * hb) // hs + h, b * nt + nt - 1 - c, 0))

    out = jax.ShapeDtypeStruct((heads, n, HEAD_DIM), BF16)
    return pl.pallas_call(
        functools.partial(_gla_body, tl=tl),
        grid=(batch, heads // hs, nt),
        in_specs=[fwd(1), fwd(2), fwd(4), bwd(1), bwd(3), bwd(4)],
        out_specs=[pl.BlockSpec((hs, tl, HEAD_DIM), lambda b, h, c: (h, b * nt + c, 0)),
                   pl.BlockSpec((hs, tl, HEAD_DIM), lambda b, h, c: (h, b * nt + nt - 1 - c, 0))],
        out_shape=[out, out],
        scratch_shapes=([pltpu.VMEM((hs, HEAD_DIM, HEAD_DIM), F32)] * 2
                        + [pltpu.VMEM((hs, tl, HEAD_DIM), F32)] * 6),
        compiler_params=_params(("parallel", "parallel", "arbitrary")),
        name="gla_scan",
    )(proj, proj, proj, proj, proj, proj)


def _conv_ln_rows(xp_scr, y_scr, u_scr, w_ref, cb_ref, lw_ref, lb_ref, slot, base, *, taps, sub):
    nb = xp_scr.shape[0]
    lane = HEAD_DIM
    ch = min(sub, 128)
    first = CONV_HALO - taps // 2
    nwin = (first + taps - 1) // 8 + 1
    for k in range(nb):
        for c0 in range(0, sub, ch):
            wins = [xp_scr[k, pl.ds(base + (c0 + 8 * a), ch + 8), :] for a in range(nwin)]
            acc = jnp.zeros((ch, lane), F32) + cb_ref[k]
            for r in range(8):
                z = None
                for a in range(nwin):
                    j = 8 * a + r - first
                    if 0 <= j < taps:
                        term = w_ref[k, pl.ds(j, 1), :] * wins[a][r:r + ch, :]
                        z = term if z is None else z + term
                if z is not None:
                    acc = acc + z
            y_scr[k, c0:c0 + ch, :] = acc
    inv_c = 1.0 / (nb * lane)
    nr = 32
    for r0 in range(0, sub, nr):
        y = y_scr[:, r0:r0 + nr, :]
        mu = jnp.sum(jnp.sum(y, axis=0), axis=-1, keepdims=True) * inv_c
        yc = y - mu
        var = jnp.sum(jnp.sum(yc * yc, axis=0), axis=-1, keepdims=True) * inv_c
        yn = yc * lax.rsqrt(var + NORM_EPS) * lw_ref[...] + lb_ref[...]
        yn = (yn * _sigmoid(yn)).astype(BF16)
        for k in range(nb):
            u_scr[slot, r0:r0 + nr, k * lane:(k + 1) * lane] = yn[k]


def _mix_rows(x_ref, of_ref, ob_ref, og_ref, hn_ref, w_ref, o_ref, u_scr, slot, base, *, heads, sub):
    rows = pl.ds(base, sub)
    parts = []
    for h in range(heads):
        o = of_ref[h, rows, :].astype(F32) + ob_ref[h, rows, :].astype(F32)
        parts.append((_rms(o, hn_ref[...]) * og_ref[h, rows, :]).astype(BF16))
    on = jnp.concatenate(parts, axis=-1)
    mixed = jnp.concatenate([u_scr[slot], on], axis=-1)
    o_ref[rows, :] = x_ref[rows, :] + jnp.dot(mixed, w_ref[...], preferred_element_type=F32)


def _mix_out_body(x_ref, c_ref, p_ref, n_ref, of_ref, ob_ref, og_ref, w_ref, cb_ref, lw_ref, lb_ref, hn_ref,
                  wout_ref, o_ref, xp_scr, y_scr, u_scr, *, tm, taps, heads, sub):
    i = pl.program_id(1)
    ni = pl.num_programs(1)
    xp_scr[:, pl.ds(CONV_HALO, tm), :] = c_ref[...]
    xp_scr[:, pl.ds(0, CONV_HALO), :] = jnp.where(i > 0, p_ref[...], 0.0)
    xp_scr[:, pl.ds(CONV_HALO + tm, CONV_HALO), :] = jnp.where(i < ni - 1, n_ref[...], 0.0)

    conv = functools.partial(_conv_ln_rows, xp_scr, y_scr, u_scr, w_ref, cb_ref, lw_ref, lb_ref, taps=taps, sub=sub)
    mix = functools.partial(_mix_rows, x_ref, of_ref, ob_ref, og_ref, hn_ref, wout_ref, o_ref, u_scr,
                            heads=heads, sub=sub)
    nsub = tm // sub
    conv(0, 0)

    def step(n, carry):
        conv((n + 1) % 2, pl.multiple_of((n + 1) * sub, sub))
        mix(n % 2, pl.multiple_of(n * sub, sub))
        return carry

    lax.fori_loop(0, nsub - 1, step, 0)
    mix((nsub - 1) % 2, (nsub - 1) * sub)


def _mix_out(x, proj, o_f, o_b, conv_w, conv_b, ln_w, ln_b, hn, w_out, *, batch, seq, heads, group):
    n, d = x.shape
    taps, cw = conv_w.shape
    hw = o_f.shape[0] * HEAD_DIM
    assert hw == group and cw == group and taps // 2 <= CONV_HALO and n == batch * seq and cw % HEAD_DIM == 0
    assert w_out.shape == (cw + hw, d)
    nb = cw // HEAD_DIM
    tm = _pick(seq, (512, 256, 128))
    sub = min(tm, 128)
    nt = seq // tm
    hb = tm // CONV_HALO
    nhb = n // CONV_HALO
    blocked = lambda t: t.reshape(t.shape[0], nb, HEAD_DIM).transpose(1, 0, 2)
    whole = lambda shape, **kw: pl.BlockSpec(shape, lambda b, i: (0,) * len(shape), **kw)
    tile = lambda width: pl.BlockSpec((tm, width), lambda b, i: (b * nt + i, 0))
    heads_tile = lambda grp=0: pl.BlockSpec((nb, tm, HEAD_DIM), lambda b, i: (grp, b * nt + i, 0))
    once = pl.Buffered(1)
    return pl.pallas_call(
        functools.partial(_mix_out_body, tm=tm, taps=taps, heads=heads, sub=sub),
        grid=(batch, nt),
        in_specs=[
            tile(d),
            heads_tile(0),
            pl.BlockSpec((nb, CONV_HALO, HEAD_DIM), lambda b, i: (0, jnp.maximum((b * nt + i) * hb - 1, 0), 0)),
            pl.BlockSpec((nb, CONV_HALO, HEAD_DIM), lambda b, i: (0, jnp.minimum((b * nt + i + 1) * hb, nhb - 1), 0)),
            heads_tile(),
            heads_tile(),
            heads_tile(5),
            whole((nb, taps, HEAD_DIM)),
            whole((nb, 1, HEAD_DIM)),
            whole((nb, 1, HEAD_DIM)),
            whole((nb, 1, HEAD_DIM)),
            whole((1, HEAD_DIM)),
            whole((cw + hw, d), pipeline_mode=once),
        ],
        out_specs=tile(d),
        out_shape=jax.ShapeDtypeStruct((n, d), F32),
        scratch_shapes=[pltpu.VMEM((nb, tm + 2 * CONV_HALO, HEAD_DIM), F32), pltpu.VMEM((nb, sub, HEAD_DIM), F32),
                        pltpu.VMEM((2, sub, cw), BF16)],
        compiler_params=_params(("parallel", "arbitrary")),
        name="mix_out",
    )(x, proj, proj, proj, o_f, o_b, proj, blocked(conv_w), blocked(conv_b), blocked(ln_w), blocked(ln_b), hn,
      w_out)


def _encoder(x, p):
    batch, seq, d = x.shape
    x = x.reshape(batch * seq, d)
    depth = p["w_in"].shape[0]
    for l in range(depth):
        cw = p["conv_w"].shape[-1]
        hw = p["lb_fwd"].shape[-1]
        heads = hw // HEAD_DIM
        assert cw == hw and p["hg_norm"].shape[-1] == HEAD_DIM
        row = lambda t: t[l].reshape(1, -1)
        x = _ffn(x, row(p["ffn1_norm"]), p["ffn1_w1"][l], p["ffn1_w3"][l], p["ffn1_w2"][l],
                 row(p["ffn1_norm"]), final=False)
        proj = _proj(x, row(p["mix_norm"]), p["w_in"][l], p["lb_fwd"], p["lb_bwd"], layer=l)
        o_f, o_b = _gla(proj, batch=batch, seq=seq, heads=heads, group=hw)
        x = _mix_out(x, proj, o_f, o_b, p["conv_w"][l], row(p["conv_b"]), row(p["conv_ln_w"]), row(p["conv_ln_b"]),
                     row(p["hg_norm"]), p["w_out"][l],
                     batch=batch, seq=seq, heads=heads, group=hw)
        last = l == depth - 1
        fn = p["final_norm"].reshape(1, -1)
        x = _ffn(x, row(p["ffn2_norm"]), p["ffn2_w1"][l], p["ffn2_w3"][l], p["ffn2_w2"][l], fn, final=last)
    if depth == 0:
        raise NotImplementedError("depth 0")
    return x.reshape(batch, seq, d)


_MATMUL_WEIGHTS = ("ffn1_w1", "ffn1_w3", "ffn1_w2", "w_in", "w_out", "ffn2_w1", "ffn2_w3", "ffn2_w2")


def kernel(x_prompt, x_sample, ffn1_norm, ffn1_w1, ffn1_w3, ffn1_w2, mix_norm, w_in, conv_w, conv_b, conv_ln_w,
           conv_ln_b, lb_fwd, lb_bwd, hg_norm, w_out, ffn2_norm, ffn2_w1, ffn2_w3, ffn2_w2, final_norm):
    p = dict(ffn1_norm=ffn1_norm, ffn1_w1=ffn1_w1, ffn1_w3=ffn1_w3, ffn1_w2=ffn1_w2, mix_norm=mix_norm, w_in=w_in,
             conv_w=conv_w, conv_b=conv_b, conv_ln_w=conv_ln_w, conv_ln_b=conv_ln_b, lb_fwd=lb_fwd, lb_bwd=lb_bwd,
             hg_norm=hg_norm, w_out=w_out, ffn2_norm=ffn2_norm, ffn2_w1=ffn2_w1, ffn2_w3=ffn2_w3, ffn2_w2=ffn2_w2,
             final_norm=final_norm)
    for name in _MATMUL_WEIGHTS:
        p[name] = p[name].astype(BF16)
    return _encoder(x_prompt, p), _encoder(x_sample, p)
```

```python
import functools

import jax
import jax.numpy as jnp
from jax import lax
from jax.experimental import pallas as pl
from jax.experimental.pallas import tpu as pltpu

NORM_EPS = 1e-6
HEAD_DIM = 128
GLA_CHUNK = 64
CONV_HALO = 16
SAFE_LOG_DECAY = 60.0
V7X_VMEM_BYTES = 64 * 1024 * 1024
V7X_VMEM_LIMIT = V7X_VMEM_BYTES - 8 * 1024 * 1024
V7X_VMEM_LIMIT_FFN = V7X_VMEM_BYTES - 4 * 1024 * 1024

F32 = jnp.float32
BF16 = jnp.bfloat16


def _sigmoid(x):
    return 1.0 / (1.0 + jnp.exp(-x))


def _rms(x, w):
    return x * lax.rsqrt(jnp.mean(x * x, axis=-1, keepdims=True) + NORM_EPS) * w


def _params(sem, vmem=V7X_VMEM_LIMIT):
    return pltpu.CompilerParams(dimension_semantics=sem, vmem_limit_bytes=vmem)


def _pick(n, prefs):
    for p in prefs:
        if n % p == 0:
            return p
    return n


def _ffn_body(x_ref, nw_ref, w1_ref, w3_ref, w2_ref, fn_ref, o_ref, h_scr, *, nj, final, sub, sub0):
    j = pl.program_id(1)
    tm = h_scr.shape[0]

    def swiglu(h):
        a = jnp.dot(h, w1_ref[...], preferred_element_type=F32)
        b = jnp.dot(h, w3_ref[...], preferred_element_type=F32)
        p = (a * _sigmoid(a) * b).astype(BF16)
        return jnp.dot(p, w2_ref[...], preferred_element_type=F32)

    def ff_block(first, last, step):
        for r0 in range(0, tm, step):
            rows = slice(r0, r0 + step)
            if first:
                h = _rms(x_ref[rows, :], nw_ref[...]).astype(BF16)
                h_scr[rows, :] = h
                acc = swiglu(h)
            else:
                acc = o_ref[rows, :] + swiglu(h_scr[rows, :])
            if last:
                acc = x_ref[rows, :] + 0.5 * acc
                if final:
                    acc = _rms(acc, fn_ref[...])
            o_ref[rows, :] = acc

    if nj == 1:
        ff_block(True, True, sub0)
    else:
        pl.when(j == 0)(lambda: ff_block(True, False, sub0))
        pl.when((j > 0) & (j < nj - 1))(lambda: ff_block(False, False, sub))
        pl.when(j == nj - 1)(lambda: ff_block(False, True, sub0))


def _ffn(x, nw, w1, w3, w2, fn, *, final):
    n, d = x.shape
    f = w1.shape[1]
    tm = _pick(n, (1024, 512, 256, 128, 64, 32, 16, 8))
    tf = _pick(f, (512, 256, 128))
    nj = f // tf
    return pl.pallas_call(
        functools.partial(_ffn_body, nj=nj, final=final, sub=min(tm, 512), sub0=min(tm, 256)),
        grid=(n // tm, nj),
        in_specs=[
            pl.BlockSpec((tm, d), lambda i, j: (i, 0)),
            pl.BlockSpec((1, d), lambda i, j: (0, 0)),
            pl.BlockSpec((d, tf), lambda i, j: (0, j)),
            pl.BlockSpec((d, tf), lambda i, j: (0, j)),
            pl.BlockSpec((tf, d), lambda i, j: (j, 0)),
            pl.BlockSpec((1, d), lambda i, j: (0, 0)),
        ],
        out_specs=pl.BlockSpec((tm, d), lambda i, j: (i, 0)),
        out_shape=jax.ShapeDtypeStruct((n, d), F32),
        scratch_shapes=[pltpu.VMEM((tm, d), BF16)],
        compiler_params=_params(("parallel", "arbitrary"), V7X_VMEM_LIMIT_FFN),
        name="ffn_final" if final else "ffn",
    )(x, nw, w1, w3, w2, fn)


def _lower_bound(lb_ref, layer):
    lb = lb_ref[...]
    e = jnp.exp(lb - jnp.max(lb, axis=0, keepdims=True))
    return jnp.sum(e[: layer + 1], axis=0, keepdims=True) / jnp.sum(e, axis=0, keepdims=True)


def _proj_body(x_ref, nw_ref, w_ref, lbf_ref, lbb_ref, o32_ref, o16_ref, *, layer, g, sub):
    hb = g // HEAD_DIM
    for r0 in range(0, x_ref.shape[0], sub):
        rows = slice(r0, r0 + sub)
        h = _rms(x_ref[rows, :], nw_ref[...]).astype(BF16)

        def group(j):
            return jnp.dot(h, w_ref[:, j * g:(j + 1) * g], preferred_element_type=F32)

        def put(o_ref, j, val):
            for hd in range(hb):
                o_ref[j * hb + hd, rows, :] = val[:, hd * HEAD_DIM:(hd + 1) * HEAD_DIM].astype(o_ref.dtype)

        put(o32_ref, 0, group(0) * _sigmoid(group(1)))
        q = group(2)
        put(o16_ref, 0, q * _sigmoid(q))
        og = group(6)
        put(o16_ref, 2, og * _sigmoid(og))
        for j_out, j_in, lb_ref in ((1, 3, lbf_ref), (2, 4, lbb_ref)):
            lb = _lower_bound(lb_ref, layer)
            put(o32_ref, j_out, jnp.log(lb + (1.0 - lb) * _sigmoid(group(j_in))))
        put(o16_ref, 1, group(5))


def _proj(x, nw, w_in, lbf, lbb, *, layer):
    n, d = x.shape
    g = lbf.shape[1]
    assert w_in.shape[1] == 7 * g
    tm = _pick(n, (512, 256, 128, 64, 32, 16))
    nl = lbf.shape[0]
    nblk = 3 * g // HEAD_DIM
    once = pl.Buffered(1)
    return pl.pallas_call(
        functools.partial(_proj_body, layer=layer, g=g, sub=min(tm, 256)),
        grid=(n // tm,),
        in_specs=[
            pl.BlockSpec((tm, d), lambda i: (i, 0)),
            pl.BlockSpec((1, d), lambda i: (0, 0)),
            pl.BlockSpec((d, 7 * g), lambda i: (0, 0), pipeline_mode=once),
            pl.BlockSpec((nl, g), lambda i: (0, 0)),
            pl.BlockSpec((nl, g), lambda i: (0, 0)),
        ],
        out_specs=[pl.BlockSpec((nblk, tm, HEAD_DIM), lambda i: (0, i, 0))] * 2,
        out_shape=[jax.ShapeDtypeStruct((nblk, n, HEAD_DIM), F32), jax.ShapeDtypeStruct((nblk, n, HEAD_DIM), BF16)],
        compiler_params=_params(("parallel",), V7X_VMEM_LIMIT_FFN),
        name="in_proj",
    )(x, nw, w_in, lbf, lbb)


def _split3(g):
    def head(x):
        bits = pltpu.bitcast(x, jnp.uint32) & jnp.uint32(0xFFFF0000)
        return pltpu.bitcast(bits, F32)
    g1 = head(g)
    r1 = g - g1
    g2 = head(r1)
    g3 = r1 - g2
    return g1.astype(BF16), g2.astype(BF16), g3.astype(BF16)


_NT = (((1,), (1,)), ((), ()))
_TN = (((0,), (0,)), ((), ()))


def _gla_direction(q_ref, g_ref, v_ref, s_scr, tcum, last, order):
    ch = GLA_CHUNK
    g = g_ref[...]
    g1, g2, g3 = _split3(g)
    pieces = [p[i * ch:(i + 1) * ch] for i in order for p in (g1, g2, g3)]
    cs = jnp.dot(tcum, jnp.concatenate(pieces, axis=1), preferred_element_type=F32)
    chunks = []
    worst = None
    for n, i in enumerate(order):
        rows = slice(i * ch, (i + 1) * ch)
        c0 = 3 * HEAD_DIM * n
        b = cs[:, c0:c0 + HEAD_DIM] + cs[:, c0 + HEAD_DIM:c0 + 2 * HEAD_DIM] + cs[:, c0 + 2 * HEAD_DIM:c0 + 3 * HEAD_DIM]
        k = 1.0 - jnp.exp(g[rows])
        bl = b[last:last + 1, :]
        worst = bl if worst is None else jnp.minimum(worst, bl)
        chunks.append(dict(rows=rows, b=b, k=k, bl=bl, qe=(q_ref[rows, :] * jnp.exp(b)).astype(BF16),
                           vb=v_ref[rows, :].astype(BF16)))
    st = s_scr[...]
    for c in chunks:
        k2 = (c["k"] * jnp.exp(c["bl"] - c["b"])).astype(BF16)
        c["prior"] = st.astype(BF16)
        st = st * jnp.exp(c["bl"]) + lax.dot_general(c["vb"], k2, _TN, preferred_element_type=F32)
    s_scr[...] = st
    return chunks, jnp.min(worst)


def _gla_pairwise(q_ref, v_ref, k_scr, b_scr, i_scr, o_ref, n_chunks, forward):
    ch = GLA_CHUNK
    t_idx = lax.broadcasted_iota(jnp.int32, (ch, 1), 0)

    def one_chunk(i, carry):
        rows = pl.ds(pl.multiple_of(i * ch, ch), ch)
        q = q_ref[rows, :]
        v = v_ref[rows, :]
        k = k_scr[rows, :]
        b = b_scr[rows, :]
        acc = jnp.zeros((ch, HEAD_DIM), F32)
        for s in range(ch):
            e = jnp.exp(jnp.minimum(b - b[s:s + 1, :], 0.0))
            w = jnp.sum(q * k[s:s + 1, :] * e, axis=-1, keepdims=True)
            seen = (t_idx >= s) if forward else (t_idx <= s)
            acc = acc + jnp.where(seen, w, 0.0) * v[s:s + 1, :]
        o_ref[rows, :] = (i_scr[rows, :] + acc).astype(o_ref.dtype)
        return carry

    lax.fori_loop(0, n_chunks, one_chunk, 0)


def _gla_body(qf_ref, gf_ref, vf_ref, qb_ref, gb_ref, vb_ref, of_ref, ob_ref,
              sf_scr, sb_scr, kf_scr, bf_scr, kb_scr, bb_scr, if_scr, ib_scr, *, tl):
    c = pl.program_id(2)

    @pl.when(c == 0)
    def _():
        sf_scr[...] = jnp.zeros_like(sf_scr)
        sb_scr[...] = jnp.zeros_like(sb_scr)

    ch = GLA_CHUNK
    nch = tl // ch
    heads = qf_ref.shape[0]
    row = lax.broadcasted_iota(jnp.int32, (ch, ch), 0)
    col = lax.broadcasted_iota(jnp.int32, (ch, ch), 1)
    lo = col <= row
    up = col >= row
    t_lo = jnp.where(lo, 1.0, 0.0).astype(BF16)
    t_up = jnp.where(up, 1.0, 0.0).astype(BF16)

    fwd, bwd, work, worst = [], [], [], None
    for h in range(heads):
        f, worst_f = _gla_direction(qf_ref.at[h], gf_ref.at[h], vf_ref.at[h], sf_scr.at[h], t_lo, ch - 1,
                                    list(range(nch)))
        b, worst_b = _gla_direction(qb_ref.at[h], gb_ref.at[h], vb_ref.at[h], sb_scr.at[h], t_up, 0,
                                    list(range(nch))[::-1])
        fwd.append(f)
        bwd.append(b)
        work += [(c, of_ref.at[h], lo) for c in f] + [(c, ob_ref.at[h], up) for c in b]
        least = jnp.minimum(worst_f, worst_b)
        worst = least if worst is None else jnp.minimum(worst, least)
    safe = worst >= -SAFE_LOG_DECAY

    def inter(c):
        return lax.dot_general(c["qe"], c["prior"], _NT, preferred_element_type=F32)

    @pl.when(safe)
    def _():
        scores = []
        for c, _, _ in work:
            ke = (c["k"] * jnp.exp(-c["b"])).astype(BF16)
            scores.append(lax.dot_general(c["qe"], ke, _NT, preferred_element_type=F32))
        inters = [inter(c) for c, _, _ in work]
        probs = [jnp.where(mask, sc, 0.0).astype(BF16) for sc, (_, _, mask) in zip(scores, work)]
        for p, o_in, (c, o_ref, _) in zip(probs, inters, work):
            o_ref[c["rows"], :] = (o_in + jnp.dot(p, c["vb"], preferred_element_type=F32)).astype(o_ref.dtype)

    @pl.when(jnp.logical_not(safe))
    def _():
        for h in range(heads):
            for chunks, k_scr, b_scr, i_scr in ((fwd[h], kf_scr, bf_scr, if_scr), (bwd[h], kb_scr, bb_scr, ib_scr)):
                for c in chunks:
                    k_scr[h, c["rows"], :] = c["k"]
                    b_scr[h, c["rows"], :] = c["b"]
                    i_scr[h, c["rows"], :] = inter(c)
            _gla_pairwise(qf_ref.at[h], vf_ref.at[h], kf_scr.at[h], bf_scr.at[h], if_scr.at[h], of_ref.at[h],
                          nch, True)
            _gla_pairwise(qb_ref.at[h], vb_ref.at[h], kb_scr.at[h], bb_scr.at[h], ib_scr.at[h], ob_ref.at[h],
                          nch, False)


def _gla(p32, p16, *, batch, seq, heads, group):
    n = p32.shape[1]
    tl = _pick(seq, (512, 256, 128, 64))
    nt = seq // tl
    hb = group // HEAD_DIM
    assert hb == heads
    hs = _pick(heads, (4, 2, 1))

    def fwd(grp):
        return pl.BlockSpec((hs, tl, HEAD_DIM), lambda b, h, c: ((grp * hb) // hs + h, b * nt + c, 0))

    def bwd(grp):
        return pl.BlockSpec((hs, tl, HEAD_DIM), lambda b, h, c: ((grp * hb) // hs + h, b * nt + nt - 1 - c, 0))

    out = jax.ShapeDtypeStruct((heads, n, HEAD_DIM), BF16)
    return pl.pallas_call(
        functools.partial(_gla_body, tl=tl),
        grid=(batch, heads // hs, nt),
        in_specs=[fwd(0), fwd(1), fwd(1), bwd(0), bwd(2), bwd(1)],
        out_specs=[pl.BlockSpec((hs, tl, HEAD_DIM), lambda b, h, c: (h, b * nt + c, 0)),
                   pl.BlockSpec((hs, tl, HEAD_DIM), lambda b, h, c: (h, b * nt + nt - 1 - c, 0))],
        out_shape=[out, out],
        scratch_shapes=([pltpu.VMEM((hs, HEAD_DIM, HEAD_DIM), F32)] * 2
                        + [pltpu.VMEM((hs, tl, HEAD_DIM), F32)] * 6),
        compiler_params=_params(("parallel", "parallel", "arbitrary")),
        name="gla_scan",
    )(p16, p32, p16, p16, p32, p16)


def _conv_ln_rows(xp_scr, y_scr, u_scr, w_ref, cb_ref, lw_ref, lb_ref, slot, base, *, taps, sub):
    nb = xp_scr.shape[0]
    lane = HEAD_DIM
    ch = min(sub, 128)
    first = CONV_HALO - taps // 2
    nwin = (first + taps - 1) // 8 + 1
    for k in range(nb):
        for c0 in range(0, sub, ch):
            wins = [xp_scr[k, pl.ds(base + (c0 + 8 * a), ch + 8), :] for a in range(nwin)]
            acc = jnp.zeros((ch, lane), F32) + cb_ref[k]
            for r in range(8):
                z = None
                for a in range(nwin):
                    j = 8 * a + r - first
                    if 0 <= j < taps:
                        term = w_ref[k, pl.ds(j, 1), :] * wins[a][r:r + ch, :]
                        z = term if z is None else z + term
                if z is not None:
                    acc = acc + z
            y_scr[k, c0:c0 + ch, :] = acc
    inv_c = 1.0 / (nb * lane)
    nr = 32
    for r0 in range(0, sub, nr):
        y = y_scr[:, r0:r0 + nr, :]
        mu = jnp.sum(jnp.sum(y, axis=0), axis=-1, keepdims=True) * inv_c
        yc = y - mu
        var = jnp.sum(jnp.sum(yc * yc, axis=0), axis=-1, keepdims=True) * inv_c
        yn = yc * lax.rsqrt(var + NORM_EPS) * lw_ref[...] + lb_ref[...]
        yn = (yn * _sigmoid(yn)).astype(BF16)
        for k in range(nb):
            u_scr[slot, r0:r0 + nr, k * lane:(k + 1) * lane] = yn[k]


def _mix_rows(x_ref, of_ref, ob_ref, og_ref, hn_ref, w_ref, o_ref, u_scr, slot, base, *, heads, sub):
    rows = pl.ds(base, sub)
    parts = []
    for h in range(heads):
        o = of_ref[h, rows, :].astype(F32) + ob_ref[h, rows, :].astype(F32)
        parts.append((_rms(o, hn_ref[...]) * og_ref[h, rows, :]).astype(BF16))
    on = jnp.concatenate(parts, axis=-1)
    mixed = jnp.concatenate([u_scr[slot], on], axis=-1)
    o_ref[rows, :] = x_ref[rows, :] + jnp.dot(mixed, w_ref[...], preferred_element_type=F32)


def _mix_out_body(x_ref, c_ref, p_ref, n_ref, of_ref, ob_ref, og_ref, w_ref, cb_ref, lw_ref, lb_ref, hn_ref,
                  wout_ref, o_ref, xp_scr, y_scr, u_scr, *, tm, taps, heads, sub):
    i = pl.program_id(1)
    ni = pl.num_programs(1)
    xp_scr[:, pl.ds(CONV_HALO, tm), :] = c_ref[...]
    xp_scr[:, pl.ds(0, CONV_HALO), :] = jnp.where(i > 0, p_ref[...], 0.0)
    xp_scr[:, pl.ds(CONV_HALO + tm, CONV_HALO), :] = jnp.where(i < ni - 1, n_ref[...], 0.0)

    conv = functools.partial(_conv_ln_rows, xp_scr, y_scr, u_scr, w_ref, cb_ref, lw_ref, lb_ref, taps=taps, sub=sub)
    mix = functools.partial(_mix_rows, x_ref, of_ref, ob_ref, og_ref, hn_ref, wout_ref, o_ref, u_scr,
                            heads=heads, sub=sub)
    nsub = tm // sub
    conv(0, 0)

    def step(n, carry):
        conv((n + 1) % 2, pl.multiple_of((n + 1) * sub, sub))
        mix(n % 2, pl.multiple_of(n * sub, sub))
        return carry

    lax.fori_loop(0, nsub - 1, step, 0)
    mix((nsub - 1) % 2, (nsub - 1) * sub)


def _mix_out(x, p32, p16, o_f, o_b, conv_w, conv_b, ln_w, ln_b, hn, w_out, *, batch, seq, heads, group):
    n, d = x.shape
    taps, cw = conv_w.shape
    hw = o_f.shape[0] * HEAD_DIM
    assert hw == group and cw == group and taps // 2 <= CONV_HALO and n == batch * seq and cw % HEAD_DIM == 0
    assert w_out.shape == (cw + hw, d)
    nb = cw // HEAD_DIM
    tm = _pick(seq, (512, 256, 128))
    sub = min(tm, 128)
    nt = seq // tm
    hb = tm // CONV_HALO
    nhb = n // CONV_HALO
    blocked = lambda t: t.reshape(t.shape[0], nb, HEAD_DIM).transpose(1, 0, 2)
    whole = lambda shape, **kw: pl.BlockSpec(shape, lambda b, i: (0,) * len(shape), **kw)
    tile = lambda width: pl.BlockSpec((tm, width), lambda b, i: (b * nt + i, 0))
    heads_tile = lambda grp=0: pl.BlockSpec((nb, tm, HEAD_DIM), lambda b, i: (grp, b * nt + i, 0))
    once = pl.Buffered(1)
    return pl.pallas_call(
        functools.partial(_mix_out_body, tm=tm, taps=taps, heads=heads, sub=sub),
        grid=(batch, nt),
        in_specs=[
            tile(d),
            heads_tile(0),
            pl.BlockSpec((nb, CONV_HALO, HEAD_DIM), lambda b, i: (0, jnp.maximum((b * nt + i) * hb - 1, 0), 0)),
            pl.BlockSpec((nb, CONV_HALO, HEAD_DIM), lambda b, i: (0, jnp.minimum((b * nt + i + 1) * hb, nhb - 1), 0)),
            heads_tile(),
            heads_tile(),
            heads_tile(2),
            whole((nb, taps, HEAD_DIM)),
            whole((nb, 1, HEAD_DIM)),
            whole((nb, 1, HEAD_DIM)),
            whole((nb, 1, HEAD_DIM)),
            whole((1, HEAD_DIM)),
            whole((cw + hw, d), pipeline_mode=once),
        ],
        out_specs=tile(d),
        out_shape=jax.ShapeDtypeStruct((n, d), F32),
        scratch_shapes=[pltpu.VMEM((nb, tm + 2 * CONV_HALO, HEAD_DIM), F32), pltpu.VMEM((nb, sub, HEAD_DIM), F32),
                        pltpu.VMEM((2, sub, cw), BF16)],
        compiler_params=_params(("parallel", "arbitrary")),
        name="mix_out",
    )(x, p32, p32, p32, o_f, o_b, p16, blocked(conv_w), blocked(conv_b), blocked(ln_w), blocked(ln_b), hn,
      w_out)


def _encoder(x, p):
    batch, seq, d = x.shape
    x = x.reshape(batch * seq, d)
    depth = p["w_in"].shape[0]
    for l in range(depth):
        cw = p["conv_w"].shape[-1]
        hw = p["lb_fwd"].shape[-1]
        heads = hw // HEAD_DIM
        assert cw == hw and p["hg_norm"].shape[-1] == HEAD_DIM
        row = lambda t: t[l].reshape(1, -1)
        x = _ffn(x, row(p["ffn1_norm"]), p["ffn1_w1"][l], p["ffn1_w3"][l], p["ffn1_w2"][l],
                 row(p["ffn1_norm"]), final=False)
        p32, p16 = _proj(x, row(p["mix_norm"]), p["w_in"][l], p["lb_fwd"], p["lb_bwd"], layer=l)
        o_f, o_b = _gla(p32, p16, batch=batch, seq=seq, heads=heads, group=hw)
        x = _mix_out(x, p32, p16, o_f, o_b, p["conv_w"][l], row(p["conv_b"]), row(p["conv_ln_w"]), row(p["conv_ln_b"]),
                     row(p["hg_norm"]), p["w_out"][l],
                     batch=batch, seq=seq, heads=heads, group=hw)
        last = l == depth - 1
        fn = p["final_norm"].reshape(1, -1)
        x = _ffn(x, row(p["ffn2_norm"]), p["ffn2_w1"][l], p["ffn2_w3"][l], p["ffn2_w2"][l], fn, final=last)
    if depth == 0:
        raise NotImplementedError("depth 0")
    return x.reshape(batch, seq, d)


_MATMUL_WEIGHTS = ("ffn1_w1", "ffn1_w3", "ffn1_w2", "w_in", "w_out", "ffn2_w1", "ffn2_w3", "ffn2_w2")


def kernel(x_prompt, x_sample, ffn1_norm, ffn1_w1, ffn1_w3, ffn1_w2, mix_norm, w_in, conv_w, conv_b, conv_ln_w,
           conv_ln_b, lb_fwd, lb_bwd, hg_norm, w_out, ffn2_norm, ffn2_w1, ffn2_w3, ffn2_w2, final_norm):
    p = dict(ffn1_norm=ffn1_norm, ffn1_w1=ffn1_w1, ffn1_w3=ffn1_w3, ffn1_w2=ffn1_w2, mix_norm=mix_norm, w_in=w_in,
             conv_w=conv_w, conv_b=conv_b, conv_ln_w=conv_ln_w, conv_ln_b=conv_ln_b, lb_fwd=lb_fwd, lb_bwd=lb_bwd,
             hg_norm=hg_norm, w_out=w_out, ffn2_norm=ffn2_norm, ffn2_w1=ffn2_w1, ffn2_w3=ffn2_w3, ffn2_w2=ffn2_w2,
             final_norm=final_norm)
    for name in _MATMUL_WEIGHTS:
        p[name] = p[name].astype(BF16)
    return _encoder(x_prompt, p), _encoder(x_sample, p)
```

```python
import functools

import jax
import jax.numpy as jnp
from jax import lax
from jax.experimental import pallas as pl
from jax.experimental.pallas import tpu as pltpu

NORM_EPS = 1e-6
HEAD_DIM = 128
GLA_CHUNK = 64
CONV_HALO = 16
SAFE_LOG_DECAY = 60.0
V7X_VMEM_BYTES = 64 * 1024 * 1024
V7X_VMEM_LIMIT = V7X_VMEM_BYTES - 8 * 1024 * 1024
V7X_VMEM_LIMIT_FFN = V7X_VMEM_BYTES - 4 * 1024 * 1024

F32 = jnp.float32
BF16 = jnp.bfloat16


def _sigmoid(x):
    return 1.0 / (1.0 + jnp.exp(-x))


def _rms(x, w):
    return x * lax.rsqrt(jnp.mean(x * x, axis=-1, keepdims=True) + NORM_EPS) * w


def _params(sem, vmem=V7X_VMEM_LIMIT):
    return pltpu.CompilerParams(dimension_semantics=sem, vmem_limit_bytes=vmem)


def _pick(n, prefs):
    for p in prefs:
        if n % p == 0:
            return p
    return n


def _ffn_body(x_ref, nw_ref, w1_ref, w3_ref, w2_ref, fn_ref, o_ref, h_scr, *, nj, final, sub, sub0):
    j = pl.program_id(1)
    tm = h_scr.shape[0]

    def swiglu(h):
        a = jnp.dot(h, w1_ref[...], preferred_element_type=F32)
        b = jnp.dot(h, w3_ref[...], preferred_element_type=F32)
        p = (a * _sigmoid(a) * b).astype(BF16)
        return jnp.dot(p, w2_ref[...], preferred_element_type=F32)

    def ff_block(first, last, step):
        for r0 in range(0, tm, step):
            rows = slice(r0, r0 + step)
            if first:
                h = _rms(x_ref[rows, :], nw_ref[...]).astype(BF16)
                h_scr[rows, :] = h
                acc = swiglu(h)
            else:
                acc = o_ref[rows, :] + swiglu(h_scr[rows, :])
            if last:
                acc = x_ref[rows, :] + 0.5 * acc
                if final:
                    acc = _rms(acc, fn_ref[...])
            o_ref[rows, :] = acc

    if nj == 1:
        ff_block(True, True, sub0)
    else:
        pl.when(j == 0)(lambda: ff_block(True, False, sub0))
        pl.when((j > 0) & (j < nj - 1))(lambda: ff_block(False, False, sub))
        pl.when(j == nj - 1)(lambda: ff_block(False, True, sub0))


def _ffn(x, nw, w1, w3, w2, fn, *, final):
    n, d = x.shape
    f = w1.shape[1]
    tm = _pick(n, (1024, 512, 256, 128, 64, 32, 16, 8))
    tf = _pick(f, (512, 256, 128))
    nj = f // tf
    return pl.pallas_call(
        functools.partial(_ffn_body, nj=nj, final=final, sub=min(tm, 512), sub0=min(tm, 256)),
        grid=(n // tm, nj),
        in_specs=[
            pl.BlockSpec((tm, d), lambda i, j: (i, 0)),
            pl.BlockSpec((1, d), lambda i, j: (0, 0)),
            pl.BlockSpec((d, tf), lambda i, j: (0, j)),
            pl.BlockSpec((d, tf), lambda i, j: (0, j)),
            pl.BlockSpec((tf, d), lambda i, j: (j, 0)),
            pl.BlockSpec((1, d), lambda i, j: (0, 0)),
        ],
        out_specs=pl.BlockSpec((tm, d), lambda i, j: (i, 0)),
        out_shape=jax.ShapeDtypeStruct((n, d), F32),
        scratch_shapes=[pltpu.VMEM((tm, d), BF16)],
        compiler_params=_params(("parallel", "arbitrary"), V7X_VMEM_LIMIT_FFN),
        name="ffn_final" if final else "ffn",
    )(x, nw, w1, w3, w2, fn)


def _lower_bound(lb_ref, layer):
    lb = lb_ref[...]
    e = jnp.exp(lb - jnp.max(lb, axis=0, keepdims=True))
    return jnp.sum(e[: layer + 1], axis=0, keepdims=True) / jnp.sum(e, axis=0, keepdims=True)


def _proj_body(x_ref, nw_ref, w_ref, lbf_ref, lbb_ref, o32_ref, o16_ref, *, layer, g, sub):
    hb = g // HEAD_DIM
    for r0 in range(0, x_ref.shape[0], sub):
        rows = slice(r0, r0 + sub)
        h = _rms(x_ref[rows, :], nw_ref[...]).astype(BF16)

        def group(j):
            return jnp.dot(h, w_ref[:, j * g:(j + 1) * g], preferred_element_type=F32)

        def put(o_ref, j, val):
            for hd in range(hb):
                o_ref[j * hb + hd, rows, :] = val[:, hd * HEAD_DIM:(hd + 1) * HEAD_DIM].astype(o_ref.dtype)

        put(o32_ref, 0, group(0) * _sigmoid(group(1)))
        q = group(2)
        put(o16_ref, 0, q * _sigmoid(q))
        og = group(6)
        put(o16_ref, 2, og * _sigmoid(og))
        for j_out, j_in, lb_ref in ((1, 3, lbf_ref), (2, 4, lbb_ref)):
            lb = _lower_bound(lb_ref, layer)
            put(o32_ref, j_out, jnp.log(lb + (1.0 - lb) * _sigmoid(group(j_in))))
        put(o16_ref, 1, group(5))


def _proj(x, nw, w_in, lbf, lbb, *, layer):
    n, d = x.shape
    g = lbf.shape[1]
    assert w_in.shape[1] == 7 * g
    tm = _pick(n, (512, 256, 128, 64, 32, 16))
    nl = lbf.shape[0]
    nblk = 3 * g // HEAD_DIM
    once = pl.Buffered(1)
    return pl.pallas_call(
        functools.partial(_proj_body, layer=layer, g=g, sub=min(tm, 256)),
        grid=(n // tm,),
        in_specs=[
            pl.BlockSpec((tm, d), lambda i: (i, 0)),
            pl.BlockSpec((1, d), lambda i: (0, 0)),
            pl.BlockSpec((d, 7 * g), lambda i: (0, 0), pipeline_mode=once),
            pl.BlockSpec((nl, g), lambda i: (0, 0)),
            pl.BlockSpec((nl, g), lambda i: (0, 0)),
        ],
        out_specs=[pl.BlockSpec((nblk, tm, HEAD_DIM), lambda i: (0, i, 0))] * 2,
        out_shape=[jax.ShapeDtypeStruct((nblk, n, HEAD_DIM), F32), jax.ShapeDtypeStruct((nblk, n, HEAD_DIM), BF16)],
        compiler_params=_params(("parallel",), V7X_VMEM_LIMIT_FFN),
        name="in_proj",
    )(x, nw, w_in, lbf, lbb)


def _split3(g):
    def head(x):
        bits = pltpu.bitcast(x, jnp.uint32) & jnp.uint32(0xFFFF0000)
        return pltpu.bitcast(bits, F32)
    g1 = head(g)
    r1 = g - g1
    g2 = head(r1)
    g3 = r1 - g2
    return g1.astype(BF16), g2.astype(BF16), g3.astype(BF16)


_NT = (((1,), (1,)), ((), ()))
_TN = (((0,), (0,)), ((), ()))


def _gla_direction(q_ref, g_ref, v_ref, s_scr, tcum, last, order):
    ch = GLA_CHUNK
    g = g_ref[...]
    g1, g2, g3 = _split3(g)
    pieces = [p[i * ch:(i + 1) * ch] for i in order for p in (g1, g2, g3)]
    cs = jnp.dot(tcum, jnp.concatenate(pieces, axis=1), preferred_element_type=F32)
    chunks = []
    worst = None
    for n, i in enumerate(order):
        rows = slice(i * ch, (i + 1) * ch)
        c0 = 3 * HEAD_DIM * n
        b = cs[:, c0:c0 + HEAD_DIM] + cs[:, c0 + HEAD_DIM:c0 + 2 * HEAD_DIM] + cs[:, c0 + 2 * HEAD_DIM:c0 + 3 * HEAD_DIM]
        k = 1.0 - jnp.exp(g[rows])
        bl = b[last:last + 1, :]
        worst = bl if worst is None else jnp.minimum(worst, bl)
        chunks.append(dict(rows=rows, b=b, k=k, bl=bl, qe=(q_ref[rows, :] * jnp.exp(b)).astype(BF16),
                           vb=v_ref[rows, :].astype(BF16)))
    st = s_scr[...]
    for c in chunks:
        k2 = (c["k"] * jnp.exp(c["bl"] - c["b"])).astype(BF16)
        c["prior"] = st.astype(BF16)
        st = st * jnp.exp(c["bl"]) + lax.dot_general(c["vb"], k2, _TN, preferred_element_type=F32)
    s_scr[...] = st
    return chunks, jnp.min(worst)


def _gla_pairwise(q_ref, v_ref, k_scr, b_scr, i_scr, o_ref, n_chunks, forward):
    ch = GLA_CHUNK
    t_idx = lax.broadcasted_iota(jnp.int32, (ch, 1), 0)

    def one_chunk(i, carry):
        rows = pl.ds(pl.multiple_of(i * ch, ch), ch)
        q = q_ref[rows, :]
        v = v_ref[rows, :]
        k = k_scr[rows, :]
        b = b_scr[rows, :]
        acc = jnp.zeros((ch, HEAD_DIM), F32)
        for s in range(ch):
            e = jnp.exp(jnp.minimum(b - b[s:s + 1, :], 0.0))
            w = jnp.sum(q * k[s:s + 1, :] * e, axis=-1, keepdims=True)
            seen = (t_idx >= s) if forward else (t_idx <= s)
            acc = acc + jnp.where(seen, w, 0.0) * v[s:s + 1, :]
        o_ref[rows, :] = (i_scr[rows, :] + acc).astype(o_ref.dtype)
        return carry

    lax.fori_loop(0, n_chunks, one_chunk, 0)


def _gla_body(qf_ref, gf_ref, vf_ref, qb_ref, gb_ref, vb_ref, of_ref, ob_ref,
              sf_scr, sb_scr, kf_scr, bf_scr, kb_scr, bb_scr, if_scr, ib_scr, *, tl):
    c = pl.program_id(2)

    @pl.when(c == 0)
    def _():
        sf_scr[...] = jnp.zeros_like(sf_scr)
        sb_scr[...] = jnp.zeros_like(sb_scr)

    ch = GLA_CHUNK
    nch = tl // ch
    heads = qf_ref.shape[0]
    row = lax.broadcasted_iota(jnp.int32, (ch, ch), 0)
    col = lax.broadcasted_iota(jnp.int32, (ch, ch), 1)
    lo = col <= row
    up = col >= row
    t_lo = jnp.where(lo, 1.0, 0.0).astype(BF16)
    t_up = jnp.where(up, 1.0, 0.0).astype(BF16)

    fwd, bwd, work, worst = [], [], [], None
    for h in range(heads):
        f, worst_f = _gla_direction(qf_ref.at[h], gf_ref.at[h], vf_ref.at[h], sf_scr.at[h], t_lo, ch - 1,
                                    list(range(nch)))
        b, worst_b = _gla_direction(qb_ref.at[h], gb_ref.at[h], vb_ref.at[h], sb_scr.at[h], t_up, 0,
                                    list(range(nch))[::-1])
        fwd.append(f)
        bwd.append(b)
        work += [(c, of_ref.at[h], lo) for c in f] + [(c, ob_ref.at[h], up) for c in b]
        least = jnp.minimum(worst_f, worst_b)
        worst = least if worst is None else jnp.minimum(worst, least)
    safe = worst >= -SAFE_LOG_DECAY

    def inter(c):
        return lax.dot_general(c["qe"], c["prior"], _NT, preferred_element_type=F32)

    @pl.when(safe)
    def _():
        scores = []
        for c, _, _ in work:
            ke = (c["k"] * jnp.exp(-c["b"])).astype(BF16)
            scores.append(lax.dot_general(c["qe"], ke, _NT, preferred_element_type=F32))
        inters = [inter(c) for c, _, _ in work]
        probs = [jnp.where(mask, sc, 0.0).astype(BF16) for sc, (_, _, mask) in zip(scores, work)]
        for p, o_in, (c, o_ref, _) in zip(probs, inters, work):
            o_ref[c["rows"], :] = (o_in + jnp.dot(p, c["vb"], preferred_element_type=F32)).astype(o_ref.dtype)

    @pl.when(jnp.logical_not(safe))
    def _():
        for h in range(heads):
            for chunks, k_scr, b_scr, i_scr in ((fwd[h], kf_scr, bf_scr, if_scr), (bwd[h], kb_scr, bb_scr, ib_scr)):
                for c in chunks:
                    k_scr[h, c["rows"], :] = c["k"]
                    b_scr[h, c["rows"], :] = c["b"]
                    i_scr[h, c["rows"], :] = inter(c)
            _gla_pairwise(qf_ref.at[h], vf_ref.at[h], kf_scr.at[h], bf_scr.at[h], if_scr.at[h], of_ref.at[h],
                          nch, True)
            _gla_pairwise(qb_ref.at[h], vb_ref.at[h], kb_scr.at[h], bb_scr.at[h], ib_scr.at[h], ob_ref.at[h],
                          nch, False)


def _gla(p32, p16, *, batch, seq, heads, group):
    n = p32.shape[1]
    tl = _pick(seq, (512, 256, 128, 64))
    nt = seq // tl
    hb = group // HEAD_DIM
    assert hb == heads
    hs = _pick(heads, (4, 2, 1))

    def fwd(grp):
        return pl.BlockSpec((hs, tl, HEAD_DIM), lambda b, h, c: ((grp * hb) // hs + h, b * nt + c, 0))

    def bwd(grp):
        return pl.BlockSpec((hs, tl, HEAD_DIM), lambda b, h, c: ((grp * hb) // hs + h, b * nt + nt - 1 - c, 0))

    out = jax.ShapeDtypeStruct((heads, n, HEAD_DIM), BF16)
    return pl.pallas_call(
        functools.partial(_gla_body, tl=tl),
        grid=(batch, heads // hs, nt),
        in_specs=[fwd(0), fwd(1), fwd(1), bwd(0), bwd(2), bwd(1)],
        out_specs=[pl.BlockSpec((hs, tl, HEAD_DIM), lambda b, h, c: (h, b * nt + c, 0)),
                   pl.BlockSpec((hs, tl, HEAD_DIM), lambda b, h, c: (h, b * nt + nt - 1 - c, 0))],
        out_shape=[out, out],
        scratch_shapes=([pltpu.VMEM((hs, HEAD_DIM, HEAD_DIM), F32)] * 2
                        + [pltpu.VMEM((hs, tl, HEAD_DIM), F32)] * 6),
        compiler_params=_params(("parallel", "parallel", "arbitrary")),
        name="gla_scan",
    )(p16, p32, p16, p16, p32, p16)


def _conv_ln_rows(xp_scr, y_scr, u_scr, w_ref, cb_ref, lw_ref, lb_ref, slot, base, *, taps, sub):
    nb = xp_scr.shape[0]
    lane = HEAD_DIM
    ch = min(sub, 128)
    first = CONV_HALO - taps // 2
    nwin = (first + taps - 1) // 8 + 1
    for k in range(nb):
        for c0 in range(0, sub, ch):
            wins = [xp_scr[k, pl.ds(base + (c0 + 8 * a), ch + 8), :] for a in range(nwin)]
            acc = jnp.zeros((ch, lane), F32) + cb_ref[k]
            for r in range(8):
                z = None
                for a in range(nwin):
                    j = 8 * a + r - first
                    if 0 <= j < taps:
                        term = w_ref[k, pl.ds(j, 1), :] * wins[a][r:r + ch, :]
                        z = term if z is None else z + term
                if z is not None:
                    acc = acc + z
            y_scr[k, c0:c0 + ch, :] = acc
    inv_c = 1.0 / (nb * lane)
    nr = 32
    for r0 in range(0, sub, nr):
        y = y_scr[:, r0:r0 + nr, :]
        mu = jnp.sum(jnp.sum(y, axis=0), axis=-1, keepdims=True) * inv_c
        yc = y - mu
        var = jnp.sum(jnp.sum(yc * yc, axis=0), axis=-1, keepdims=True) * inv_c
        yn = yc * lax.rsqrt(var + NORM_EPS) * lw_ref[...] + lb_ref[...]
        yn = (yn * _sigmoid(yn)).astype(BF16)
        for k in range(nb):
            u_scr[slot, r0:r0 + nr, k * lane:(k + 1) * lane] = yn[k]


def _mix_rows(x_ref, of_ref, ob_ref, og_ref, hn_ref, w_ref, o_ref, u_scr, slot, base, *, heads, sub):
    rows = pl.ds(base, sub)
    parts = []
    for h in range(heads):
        o = of_ref[h, rows, :].astype(F32) + ob_ref[h, rows, :].astype(F32)
        parts.append((_rms(o, hn_ref[...]) * og_ref[h, rows, :]).astype(BF16))
    on = jnp.concatenate(parts, axis=-1)
    mixed = jnp.concatenate([u_scr[slot], on], axis=-1)
    o_ref[rows, :] = x_ref[rows, :] + jnp.dot(mixed, w_ref[...], preferred_element_type=F32)


def _mix_out_body(x_ref, c_ref, p_ref, n_ref, of_ref, ob_ref, og_ref, w_ref, cb_ref, lw_ref, lb_ref, hn_ref,
                  wout_ref, o_ref, xp_scr, y_scr, u_scr, *, tm, taps, heads, sub):
    i = pl.program_id(1)
    ni = pl.num_programs(1)
    xp_scr[:, pl.ds(CONV_HALO, tm), :] = c_ref[...]
    xp_scr[:, pl.ds(0, CONV_HALO), :] = jnp.where(i > 0, p_ref[...], 0.0)
    xp_scr[:, pl.ds(CONV_HALO + tm, CONV_HALO), :] = jnp.where(i < ni - 1, n_ref[...], 0.0)

    conv = functools.partial(_conv_ln_rows, xp_scr, y_scr, u_scr, w_ref, cb_ref, lw_ref, lb_ref, taps=taps, sub=sub)
    mix = functools.partial(_mix_rows, x_ref, of_ref, ob_ref, og_ref, hn_ref, wout_ref, o_ref, u_scr,
                            heads=heads, sub=sub)
    nsub = tm // sub
    conv(0, 0)

    def step(n, carry):
        conv((n + 1) % 2, pl.multiple_of((n + 1) * sub, sub))
        mix(n % 2, pl.multiple_of(n * sub, sub))
        return carry

    lax.fori_loop(0, nsub - 1, step, 0)
    mix((nsub - 1) % 2, (nsub - 1) * sub)


def _mix_out(x, p32, p16, o_f, o_b, conv_w, conv_b, ln_w, ln_b, hn, w_out, *, batch, seq, heads, group):
    n, d = x.shape
    taps, cw = conv_w.shape
    hw = o_f.shape[0] * HEAD_DIM
    assert hw == group and cw == group and taps // 2 <= CONV_HALO and n == batch * seq and cw % HEAD_DIM == 0
    assert w_out.shape == (cw + hw, d)
    nb = cw // HEAD_DIM
    tm = _pick(seq, (512, 256, 128))
    sub = min(tm, 128)
    nt = seq // tm
    hb = tm // CONV_HALO
    nhb = n // CONV_HALO
    blocked = lambda t: t.reshape(t.shape[0], nb, HEAD_DIM).transpose(1, 0, 2)
    whole = lambda shape, **kw: pl.BlockSpec(shape, lambda b, i: (0,) * len(shape), **kw)
    tile = lambda width: pl.BlockSpec((tm, width), lambda b, i: (b * nt + i, 0))
    heads_tile = lambda grp=0: pl.BlockSpec((nb, tm, HEAD_DIM), lambda b, i: (grp, b * nt + i, 0))
    once = pl.Buffered(1)
    return pl.pallas_call(
        functools.partial(_mix_out_body, tm=tm, taps=taps, heads=heads, sub=sub),
        grid=(batch, nt),
        in_specs=[
            tile(d),
            heads_tile(0),
            pl.BlockSpec((nb, CONV_HALO, HEAD_DIM), lambda b, i: (0, jnp.maximum((b * nt + i) * hb - 1, 0), 0)),
            pl.BlockSpec((nb, CONV_HALO, HEAD_DIM), lambda b, i: (0, jnp.minimum((b * nt + i + 1) * hb, nhb - 1), 0)),
            heads_tile(),
            heads_tile(),
            heads_tile(2),
            whole((nb, taps, HEAD_DIM)),
            whole((nb, 1, HEAD_DIM)),
            whole((nb, 1, HEAD_DIM)),
            whole((nb, 1, HEAD_DIM)),
            whole((1, HEAD_DIM)),
            whole((cw + hw, d), pipeline_mode=once),
        ],
        out_specs=tile(d),
        out_shape=jax.ShapeDtypeStruct((n, d), F32),
        scratch_shapes=[pltpu.VMEM((nb, tm + 2 * CONV_HALO, HEAD_DIM), F32), pltpu.VMEM((nb, sub, HEAD_DIM), F32),
                        pltpu.VMEM((2, sub, cw), BF16)],
        compiler_params=_params(("parallel", "arbitrary")),
        name="mix_out",
    )(x, p32, p32, p32, o_f, o_b, p16, blocked(conv_w), blocked(conv_b), blocked(ln_w), blocked(ln_b), hn,
      w_out)


def _encoder(x, p):
    batch, seq, d = x.shape
    x = x.reshape(batch * seq, d)
    depth = p["w_in"].shape[0]
    for l in range(depth):
        cw = p["conv_w"].shape[-1]
        hw = p["lb_fwd"].shape[-1]
        heads = hw // HEAD_DIM
        assert cw == hw and p["hg_norm"].shape[-1] == HEAD_DIM
        row = lambda t: t[l].reshape(1, -1)
        x = _ffn(x, row(p["ffn1_norm"]), p["ffn1_w1"][l], p["ffn1_w3"][l], p["ffn1_w2"][l],
                 row(p["ffn1_norm"]), final=False)
        p32, p16 = _proj(x, row(p["mix_norm"]), p["w_in"][l], p["lb_fwd"], p["lb_bwd"], layer=l)
        o_f, o_b = _gla(p32, p16, batch=batch, seq=seq, heads=heads, group=hw)
        x = _mix_out(x, p32, p16, o_f, o_b, p["conv_w"][l], row(p["conv_b"]), row(p["conv_ln_w"]), row(p["conv_ln_b"]),
                     row(p["hg_norm"]), p["w_out"][l],
                     batch=batch, seq=seq, heads=heads, group=hw)
        last = l == depth - 1
        fn = p["final_norm"].reshape(1, -1)
        x = _ffn(x, row(p["ffn2_norm"]), p["ffn2_w1"][l], p["ffn2_w3"][l], p["ffn2_w2"][l], fn, final=last)
    if depth == 0:
        raise NotImplementedError("depth 0")
    return x.reshape(batch, seq, d)


_MATMUL_WEIGHTS = ("ffn1_w1", "ffn1_w3", "ffn1_w2", "w_in", "w_out", "ffn2_w1", "ffn2_w3", "ffn2_w2")
CAST_BLOCK_BYTES = 8 * 1024 * 1024


def _cast_body(x_ref, o_ref):
    o_ref[...] = x_ref[...].astype(o_ref.dtype)


def _to_bf16(w):
    layers, r, c = w.shape
    tr = next((t for t in (2048, 1408, 1024, 704, 512, 256, 128, 64, 32, 16)
               if r % t == 0 and t * c * 4 <= CAST_BLOCK_BYTES), r)
    spec = pl.BlockSpec((None, tr, c), lambda l, i: (l, i, 0))
    return pl.pallas_call(
        _cast_body,
        grid=(layers, r // tr),
        in_specs=[spec],
        out_specs=spec,
        out_shape=jax.ShapeDtypeStruct(w.shape, BF16),
        compiler_params=_params(("parallel", "parallel")),
        name="to_bf16",
    )(w)


def kernel(x_prompt, x_sample, ffn1_norm, ffn1_w1, ffn1_w3, ffn1_w2, mix_norm, w_in, conv_w, conv_b, conv_ln_w,
           conv_ln_b, lb_fwd, lb_bwd, hg_norm, w_out, ffn2_norm, ffn2_w1, ffn2_w3, ffn2_w2, final_norm):
    p = dict(ffn1_norm=ffn1_norm, ffn1_w1=ffn1_w1, ffn1_w3=ffn1_w3, ffn1_w2=ffn1_w2, mix_norm=mix_norm, w_in=w_in,
             conv_w=conv_w, conv_b=conv_b, conv_ln_w=conv_ln_w, conv_ln_b=conv_ln_b, lb_fwd=lb_fwd, lb_bwd=lb_bwd,
             hg_norm=hg_norm, w_out=w_out, ffn2_norm=ffn2_norm, ffn2_w1=ffn2_w1, ffn2_w3=ffn2_w3, ffn2_w2=ffn2_w2,
             final_norm=final_norm)
    for name in _MATMUL_WEIGHTS:
        p[name] = _to_bf16(p[name])
    return _encoder(x_prompt, p), _encoder(x_sample, p)
```

```python
import functools

import jax
import jax.numpy as jnp
from jax import lax
from jax.experimental import pallas as pl
from jax.experimental.pallas import tpu as pltpu

NORM_EPS = 1e-6
HEAD_DIM = 128
GLA_CHUNK = 64
CONV_HALO = 16
SAFE_LOG_DECAY = 60.0
V7X_VMEM_BYTES = 64 * 1024 * 1024
V7X_VMEM_LIMIT = V7X_VMEM_BYTES - 8 * 1024 * 1024
V7X_VMEM_LIMIT_FFN = V7X_VMEM_BYTES - 4 * 1024 * 1024

F32 = jnp.float32
BF16 = jnp.bfloat16


def _sigmoid(x):
    return 1.0 / (1.0 + jnp.exp(-x))


def _rms(x, w):
    return x * lax.rsqrt(jnp.mean(x * x, axis=-1, keepdims=True) + NORM_EPS) * w


def _params(sem, vmem=V7X_VMEM_LIMIT):
    return pltpu.CompilerParams(dimension_semantics=sem, vmem_limit_bytes=vmem)


def _pick(n, prefs):
    for p in prefs:
        if n % p == 0:
            return p
    return n


def _ffn_body(x_ref, nw_ref, w1_ref, w3_ref, w2_ref, fn_ref, o_ref, h_scr, *, nj, final, sub, sub0):
    j = pl.program_id(1)
    tm = h_scr.shape[0]

    def swiglu(h):
        a = jnp.dot(h, w1_ref[...], preferred_element_type=F32)
        b = jnp.dot(h, w3_ref[...], preferred_element_type=F32)
        p = (a * _sigmoid(a) * b).astype(BF16)
        return jnp.dot(p, w2_ref[...], preferred_element_type=F32)

    def ff_block(first, last, step):
        for r0 in range(0, tm, step):
            rows = slice(r0, r0 + step)
            if first:
                h = _rms(x_ref[rows, :], nw_ref[...]).astype(BF16)
                h_scr[rows, :] = h
                acc = swiglu(h)
            else:
                acc = o_ref[rows, :] + swiglu(h_scr[rows, :])
            if last:
                acc = x_ref[rows, :] + 0.5 * acc
                if final:
                    acc = _rms(acc, fn_ref[...])
            o_ref[rows, :] = acc

    if nj == 1:
        ff_block(True, True, sub0)
    else:
        pl.when(j == 0)(lambda: ff_block(True, False, sub0))
        pl.when((j > 0) & (j < nj - 1))(lambda: ff_block(False, False, sub))
        pl.when(j == nj - 1)(lambda: ff_block(False, True, sub0))


def _ffn(x, nw, w1, w3, w2, fn, *, final):
    n, d = x.shape
    f = w1.shape[1]
    tm = _pick(n, (1024, 512, 256, 128, 64, 32, 16, 8))
    tf = _pick(f, (512, 256, 128))
    nj = f // tf
    return pl.pallas_call(
        functools.partial(_ffn_body, nj=nj, final=final, sub=min(tm, 512), sub0=min(tm, 256)),
        grid=(n // tm, nj),
        in_specs=[
            pl.BlockSpec((tm, d), lambda i, j: (i, 0)),
            pl.BlockSpec((1, d), lambda i, j: (0, 0)),
            pl.BlockSpec((d, tf), lambda i, j: (0, j)),
            pl.BlockSpec((d, tf), lambda i, j: (0, j)),
            pl.BlockSpec((tf, d), lambda i, j: (j, 0)),
            pl.BlockSpec((1, d), lambda i, j: (0, 0)),
        ],
        out_specs=pl.BlockSpec((tm, d), lambda i, j: (i, 0)),
        out_shape=jax.ShapeDtypeStruct((n, d), F32),
        scratch_shapes=[pltpu.VMEM((tm, d), BF16)],
        compiler_params=_params(("parallel", "arbitrary"), V7X_VMEM_LIMIT_FFN),
        name="ffn_final" if final else "ffn",
    )(x, nw, w1, w3, w2, fn)


def _lower_bound(lb_ref, layer):
    lb = lb_ref[...]
    e = jnp.exp(lb - jnp.max(lb, axis=0, keepdims=True))
    return jnp.sum(e[: layer + 1], axis=0, keepdims=True) / jnp.sum(e, axis=0, keepdims=True)


def _proj_body(x_ref, nw_ref, w_ref, lbf_ref, lbb_ref, o32_ref, o16_ref, *, layer, g, sub):
    hb = g // HEAD_DIM
    for r0 in range(0, x_ref.shape[0], sub):
        rows = slice(r0, r0 + sub)
        h = _rms(x_ref[rows, :], nw_ref[...]).astype(BF16)

        def group(j):
            return jnp.dot(h, w_ref[:, j * g:(j + 1) * g], preferred_element_type=F32)

        def put(o_ref, j, val):
            for hd in range(hb):
                o_ref[j * hb + hd, rows, :] = val[:, hd * HEAD_DIM:(hd + 1) * HEAD_DIM].astype(o_ref.dtype)

        put(o32_ref, 0, group(0) * _sigmoid(group(1)))
        q = group(2)
        put(o16_ref, 0, q * _sigmoid(q))
        og = group(6)
        put(o16_ref, 2, og * _sigmoid(og))
        for j_out, j_in, lb_ref in ((1, 3, lbf_ref), (2, 4, lbb_ref)):
            lb = _lower_bound(lb_ref, layer)
            put(o32_ref, j_out, jnp.log(lb + (1.0 - lb) * _sigmoid(group(j_in))))
        put(o16_ref, 1, group(5))


def _proj(x, nw, w_in, lbf, lbb, *, layer):
    n, d = x.shape
    g = lbf.shape[1]
    assert w_in.shape[1] == 7 * g
    tm = _pick(n, (512, 256, 128, 64, 32, 16))
    nl = lbf.shape[0]
    nblk = 3 * g // HEAD_DIM
    once = pl.Buffered(1)
    return pl.pallas_call(
        functools.partial(_proj_body, layer=layer, g=g, sub=min(tm, 256)),
        grid=(n // tm,),
        in_specs=[
            pl.BlockSpec((tm, d), lambda i: (i, 0)),
            pl.BlockSpec((1, d), lambda i: (0, 0)),
            pl.BlockSpec((d, 7 * g), lambda i: (0, 0), pipeline_mode=once),
            pl.BlockSpec((nl, g), lambda i: (0, 0)),
            pl.BlockSpec((nl, g), lambda i: (0, 0)),
        ],
        out_specs=[pl.BlockSpec((nblk, tm, HEAD_DIM), lambda i: (0, i, 0))] * 2,
        out_shape=[jax.ShapeDtypeStruct((nblk, n, HEAD_DIM), F32), jax.ShapeDtypeStruct((nblk, n, HEAD_DIM), BF16)],
        compiler_params=_params(("parallel",), V7X_VMEM_LIMIT_FFN),
        name="in_proj",
    )(x, nw, w_in, lbf, lbb)


def _split3(g):
    def head(x):
        bits = pltpu.bitcast(x, jnp.uint32) & jnp.uint32(0xFFFF0000)
        return pltpu.bitcast(bits, F32)
    g1 = head(g)
    r1 = g - g1
    g2 = head(r1)
    g3 = r1 - g2
    return g1.astype(BF16), g2.astype(BF16), g3.astype(BF16)


_NT = (((1,), (1,)), ((), ()))
_TN = (((0,), (0,)), ((), ()))


def _gla_direction(q_ref, g_ref, v_ref, s_scr, tcum, last, order):
    ch = GLA_CHUNK
    g = g_ref[...]
    g1, g2, g3 = _split3(g)
    pieces = [p[i * ch:(i + 1) * ch] for i in order for p in (g1, g2, g3)]
    cs = jnp.dot(tcum, jnp.concatenate(pieces, axis=1), preferred_element_type=F32)
    chunks = []
    worst = None
    for n, i in enumerate(order):
        rows = slice(i * ch, (i + 1) * ch)
        c0 = 3 * HEAD_DIM * n
        b = cs[:, c0:c0 + HEAD_DIM] + cs[:, c0 + HEAD_DIM:c0 + 2 * HEAD_DIM] + cs[:, c0 + 2 * HEAD_DIM:c0 + 3 * HEAD_DIM]
        k = 1.0 - jnp.exp(g[rows])
        bl = b[last:last + 1, :]
        worst = bl if worst is None else jnp.minimum(worst, bl)
        chunks.append(dict(rows=rows, b=b, k=k, bl=bl, qe=(q_ref[rows, :] * jnp.exp(b)).astype(BF16),
                           vb=v_ref[rows, :].astype(BF16)))
    st = s_scr[...]
    for c in chunks:
        k2 = (c["k"] * jnp.exp(c["bl"] - c["b"])).astype(BF16)
        c["prior"] = st.astype(BF16)
        st = st * jnp.exp(c["bl"]) + lax.dot_general(c["vb"], k2, _TN, preferred_element_type=F32)
    s_scr[...] = st
    return chunks, jnp.min(worst)


def _gla_pairwise(q_ref, v_ref, k_scr, b_scr, i_scr, o_ref, n_chunks, forward):
    ch = GLA_CHUNK
    t_idx = lax.broadcasted_iota(jnp.int32, (ch, 1), 0)

    def one_chunk(i, carry):
        rows = pl.ds(pl.multiple_of(i * ch, ch), ch)
        q = q_ref[rows, :]
        v = v_ref[rows, :]
        k = k_scr[rows, :]
        b = b_scr[rows, :]
        acc = jnp.zeros((ch, HEAD_DIM), F32)
        for s in range(ch):
            e = jnp.exp(jnp.minimum(b - b[s:s + 1, :], 0.0))
            w = jnp.sum(q * k[s:s + 1, :] * e, axis=-1, keepdims=True)
            seen = (t_idx >= s) if forward else (t_idx <= s)
            acc = acc + jnp.where(seen, w, 0.0) * v[s:s + 1, :]
        o_ref[rows, :] = (i_scr[rows, :] + acc).astype(o_ref.dtype)
        return carry

    lax.fori_loop(0, n_chunks, one_chunk, 0)


def _gla_body(qf_ref, gf_ref, vf_ref, qb_ref, gb_ref, vb_ref, of_ref, ob_ref,
              sf_scr, sb_scr, kf_scr, bf_scr, kb_scr, bb_scr, if_scr, ib_scr, *, tl):
    c = pl.program_id(2)

    @pl.when(c == 0)
    def _():
        sf_scr[...] = jnp.zeros_like(sf_scr)
        sb_scr[...] = jnp.zeros_like(sb_scr)

    ch = GLA_CHUNK
    nch = tl // ch
    heads = qf_ref.shape[0]
    row = lax.broadcasted_iota(jnp.int32, (ch, ch), 0)
    col = lax.broadcasted_iota(jnp.int32, (ch, ch), 1)
    lo = col <= row
    up = col >= row
    t_lo = jnp.where(lo, 1.0, 0.0).astype(BF16)
    t_up = jnp.where(up, 1.0, 0.0).astype(BF16)

    fwd, bwd, work, worst = [], [], [], None
    for h in range(heads):
        f, worst_f = _gla_direction(qf_ref.at[h], gf_ref.at[h], vf_ref.at[h], sf_scr.at[h], t_lo, ch - 1,
                                    list(range(nch)))
        b, worst_b = _gla_direction(qb_ref.at[h], gb_ref.at[h], vb_ref.at[h], sb_scr.at[h], t_up, 0,
                                    list(range(nch))[::-1])
        fwd.append(f)
        bwd.append(b)
        work += [(c, of_ref.at[h], lo) for c in f] + [(c, ob_ref.at[h], up) for c in b]
        least = jnp.minimum(worst_f, worst_b)
        worst = least if worst is None else jnp.minimum(worst, least)
    safe = worst >= -SAFE_LOG_DECAY

    def inter(c):
        return lax.dot_general(c["qe"], c["prior"], _NT, preferred_element_type=F32)

    @pl.when(safe)
    def _():
        scores = []
        for c, _, _ in work:
            ke = (c["k"] * jnp.exp(-c["b"])).astype(BF16)
            scores.append(lax.dot_general(c["qe"], ke, _NT, preferred_element_type=F32))
        inters = [inter(c) for c, _, _ in work]
        probs = [jnp.where(mask, sc, 0.0).astype(BF16) for sc, (_, _, mask) in zip(scores, work)]
        for p, o_in, (c, o_ref, _) in zip(probs, inters, work):
            o_ref[c["rows"], :] = (o_in + jnp.dot(p, c["vb"], preferred_element_type=F32)).astype(o_ref.dtype)

    @pl.when(jnp.logical_not(safe))
    def _():
        for h in range(heads):
            for chunks, k_scr, b_scr, i_scr in ((fwd[h], kf_scr, bf_scr, if_scr), (bwd[h], kb_scr, bb_scr, ib_scr)):
                for c in chunks:
                    k_scr[h, c["rows"], :] = c["k"]
                    b_scr[h, c["rows"], :] = c["b"]
                    i_scr[h, c["rows"], :] = inter(c)
            _gla_pairwise(qf_ref.at[h], vf_ref.at[h], kf_scr.at[h], bf_scr.at[h], if_scr.at[h], of_ref.at[h],
                          nch, True)
            _gla_pairwise(qb_ref.at[h], vb_ref.at[h], kb_scr.at[h], bb_scr.at[h], ib_scr.at[h], ob_ref.at[h],
                          nch, False)


def _gla(p32, p16, *, batch, seq, heads, group):
    n = p32.shape[1]
    tl = _pick(seq, (512, 256, 128, 64))
    nt = seq // tl
    hb = group // HEAD_DIM
    assert hb == heads
    hs = _pick(heads, (8, 4, 2, 1))

    def fwd(grp):
        return pl.BlockSpec((hs, tl, HEAD_DIM), lambda b, h, c: ((grp * hb) // hs + h, b * nt + c, 0))

    def bwd(grp):
        return pl.BlockSpec((hs, tl, HEAD_DIM), lambda b, h, c: ((grp * hb) // hs + h, b * nt + nt - 1 - c, 0))

    out = jax.ShapeDtypeStruct((heads, n, HEAD_DIM), BF16)
    return pl.pallas_call(
        functools.partial(_gla_body, tl=tl),
        grid=(batch, heads // hs, nt),
        in_specs=[fwd(0), fwd(1), fwd(1), bwd(0), bwd(2), bwd(1)],
        out_specs=[pl.BlockSpec((hs, tl, HEAD_DIM), lambda b, h, c: (h, b * nt + c, 0)),
                   pl.BlockSpec((hs, tl, HEAD_DIM), lambda b, h, c: (h, b * nt + nt - 1 - c, 0))],
        out_shape=[out, out],
        scratch_shapes=([pltpu.VMEM((hs, HEAD_DIM, HEAD_DIM), F32)] * 2
                        + [pltpu.VMEM((hs, tl, HEAD_DIM), F32)] * 6),
        compiler_params=_params(("parallel", "parallel", "arbitrary")),
        name="gla_scan",
    )(p16, p32, p16, p16, p32, p16)


def _conv_ln_rows(xp_scr, y_scr, u_scr, w_ref, cb_ref, lw_ref, lb_ref, slot, base, *, taps, sub):
    nb = xp_scr.shape[0]
    lane = HEAD_DIM
    ch = min(sub, 128)
    first = CONV_HALO - taps // 2
    nwin = (first + taps - 1) // 8 + 1
    for k in range(nb):
        for c0 in range(0, sub, ch):
            wins = [xp_scr[k, pl.ds(base + (c0 + 8 * a), ch + 8), :] for a in range(nwin)]
            acc = jnp.zeros((ch, lane), F32) + cb_ref[k]
            for r in range(8):
                z = None
                for a in range(nwin):
                    j = 8 * a + r - first
                    if 0 <= j < taps:
                        term = w_ref[k, pl.ds(j, 1), :] * wins[a][r:r + ch, :]
                        z = term if z is None else z + term
                if z is not None:
                    acc = acc + z
            y_scr[k, c0:c0 + ch, :] = acc
    inv_c = 1.0 / (nb * lane)
    nr = 32
    for r0 in range(0, sub, nr):
        y = y_scr[:, r0:r0 + nr, :]
        mu = jnp.sum(jnp.sum(y, axis=0), axis=-1, keepdims=True) * inv_c
        yc = y - mu
        var = jnp.sum(jnp.sum(yc * yc, axis=0), axis=-1, keepdims=True) * inv_c
        yn = yc * lax.rsqrt(var + NORM_EPS) * lw_ref[...] + lb_ref[...]
        yn = (yn * _sigmoid(yn)).astype(BF16)
        for k in range(nb):
            u_scr[slot, r0:r0 + nr, k * lane:(k + 1) * lane] = yn[k]


def _mix_rows(x_ref, of_ref, ob_ref, og_ref, hn_ref, w_ref, o_ref, u_scr, slot, base, *, heads, sub):
    rows = pl.ds(base, sub)
    parts = []
    for h in range(heads):
        o = of_ref[h, rows, :].astype(F32) + ob_ref[h, rows, :].astype(F32)
        parts.append((_rms(o, hn_ref[...]) * og_ref[h, rows, :]).astype(BF16))
    on = jnp.concatenate(parts, axis=-1)
    mixed = jnp.concatenate([u_scr[slot], on], axis=-1)
    o_ref[rows, :] = x_ref[rows, :] + jnp.dot(mixed, w_ref[...], preferred_element_type=F32)


def _mix_out_body(x_ref, c_ref, p_ref, n_ref, of_ref, ob_ref, og_ref, w_ref, cb_ref, lw_ref, lb_ref, hn_ref,
                  wout_ref, o_ref, xp_scr, y_scr, u_scr, *, tm, taps, heads, sub):
    i = pl.program_id(1)
    ni = pl.num_programs(1)
    xp_scr[:, pl.ds(CONV_HALO, tm), :] = c_ref[...]
    xp_scr[:, pl.ds(0, CONV_HALO), :] = jnp.where(i > 0, p_ref[...], 0.0)
    xp_scr[:, pl.ds(CONV_HALO + tm, CONV_HALO), :] = jnp.where(i < ni - 1, n_ref[...], 0.0)

    conv = functools.partial(_conv_ln_rows, xp_scr, y_scr, u_scr, w_ref, cb_ref, lw_ref, lb_ref, taps=taps, sub=sub)
    mix = functools.partial(_mix_rows, x_ref, of_ref, ob_ref, og_ref, hn_ref, wout_ref, o_ref, u_scr,
                            heads=heads, sub=sub)
    nsub = tm // sub
    conv(0, 0)

    def step(n, carry):
        conv((n + 1) % 2, pl.multiple_of((n + 1) * sub, sub))
        mix(n % 2, pl.multiple_of(n * sub, sub))
        return carry

    lax.fori_loop(0, nsub - 1, step, 0)
    mix((nsub - 1) % 2, (nsub - 1) * sub)


def _mix_out(x, p32, p16, o_f, o_b, conv_w, conv_b, ln_w, ln_b, hn, w_out, *, batch, seq, heads, group):
    n, d = x.shape
    taps, cw = conv_w.shape
    hw = o_f.shape[0] * HEAD_DIM
    assert hw == group and cw == group and taps // 2 <= CONV_HALO and n == batch * seq and cw % HEAD_DIM == 0
    assert w_out.shape == (cw + hw, d)
    nb = cw // HEAD_DIM
    tm = _pick(seq, (512, 256, 128))
    sub = min(tm, 128)
    nt = seq // tm
    hb = tm // CONV_HALO
    nhb = n // CONV_HALO
    blocked = lambda t: t.reshape(t.shape[0], nb, HEAD_DIM).transpose(1, 0, 2)
    whole = lambda shape, **kw: pl.BlockSpec(shape, lambda b, i: (0,) * len(shape), **kw)
    tile = lambda width: pl.BlockSpec((tm, width), lambda b, i: (b * nt + i, 0))
    heads_tile = lambda grp=0: pl.BlockSpec((nb, tm, HEAD_DIM), lambda b, i: (grp, b * nt + i, 0))
    once = pl.Buffered(1)
    return pl.pallas_call(
        functools.partial(_mix_out_body, tm=tm, taps=taps, heads=heads, sub=sub),
        grid=(batch, nt),
        in_specs=[
            tile(d),
            heads_tile(0),
            pl.BlockSpec((nb, CONV_HALO, HEAD_DIM), lambda b, i: (0, jnp.maximum((b * nt + i) * hb - 1, 0), 0)),
            pl.BlockSpec((nb, CONV_HALO, HEAD_DIM), lambda b, i: (0, jnp.minimum((b * nt + i + 1) * hb, nhb - 1), 0)),
            heads_tile(),
            heads_tile(),
            heads_tile(2),
            whole((nb, taps, HEAD_DIM)),
            whole((nb, 1, HEAD_DIM)),
            whole((nb, 1, HEAD_DIM)),
            whole((nb, 1, HEAD_DIM)),
            whole((1, HEAD_DIM)),
            whole((cw + hw, d), pipeline_mode=once),
        ],
        out_specs=tile(d),
        out_shape=jax.ShapeDtypeStruct((n, d), F32),
        scratch_shapes=[pltpu.VMEM((nb, tm + 2 * CONV_HALO, HEAD_DIM), F32), pltpu.VMEM((nb, sub, HEAD_DIM), F32),
                        pltpu.VMEM((2, sub, cw), BF16)],
        compiler_params=_params(("parallel", "arbitrary")),
        name="mix_out",
    )(x, p32, p32, p32, o_f, o_b, p16, blocked(conv_w), blocked(conv_b), blocked(ln_w), blocked(ln_b), hn,
      w_out)


def _encoder(x, p):
    batch, seq, d = x.shape
    x = x.reshape(batch * seq, d)
    depth = p["w_in"].shape[0]
    for l in range(depth):
        cw = p["conv_w"].shape[-1]
        hw = p["lb_fwd"].shape[-1]
        heads = hw // HEAD_DIM
        assert cw == hw and p["hg_norm"].shape[-1] == HEAD_DIM
        row = lambda t: t[l].reshape(1, -1)
        x = _ffn(x, row(p["ffn1_norm"]), p["ffn1_w1"][l], p["ffn1_w3"][l], p["ffn1_w2"][l],
                 row(p["ffn1_norm"]), final=False)
        p32, p16 = _proj(x, row(p["mix_norm"]), p["w_in"][l], p["lb_fwd"], p["lb_bwd"], layer=l)
        o_f, o_b = _gla(p32, p16, batch=batch, seq=seq, heads=heads, group=hw)
        x = _mix_out(x, p32, p16, o_f, o_b, p["conv_w"][l], row(p["conv_b"]), row(p["conv_ln_w"]), row(p["conv_ln_b"]),
                     row(p["hg_norm"]), p["w_out"][l],
                     batch=batch, seq=seq, heads=heads, group=hw)
        last = l == depth - 1
        fn = p["final_norm"].reshape(1, -1)
        x = _ffn(x, row(p["ffn2_norm"]), p["ffn2_w1"][l], p["ffn2_w3"][l], p["ffn2_w2"][l], fn, final=last)
    if depth == 0:
        raise NotImplementedError("depth 0")
    return x.reshape(batch, seq, d)


_MATMUL_WEIGHTS = ("ffn1_w1", "ffn1_w3", "ffn1_w2", "w_in", "w_out", "ffn2_w1", "ffn2_w3", "ffn2_w2")


def kernel(x_prompt, x_sample, ffn1_norm, ffn1_w1, ffn1_w3, ffn1_w2, mix_norm, w_in, conv_w, conv_b, conv_ln_w,
           conv_ln_b, lb_fwd, lb_bwd, hg_norm, w_out, ffn2_norm, ffn2_w1, ffn2_w3, ffn2_w2, final_norm):
    p = dict(ffn1_norm=ffn1_norm, ffn1_w1=ffn1_w1, ffn1_w3=ffn1_w3, ffn1_w2=ffn1_w2, mix_norm=mix_norm, w_in=w_in,
             conv_w=conv_w, conv_b=conv_b, conv_ln_w=conv_ln_w, conv_ln_b=conv_ln_b, lb_fwd=lb_fwd, lb_bwd=lb_bwd,
             hg_norm=hg_norm, w_out=w_out, ffn2_norm=ffn2_norm, ffn2_w1=ffn2_w1, ffn2_w3=ffn2_w3, ffn2_w2=ffn2_w2,
             final_norm=final_norm)
    for name in _MATMUL_WEIGHTS:
        p[name] = p[name].astype(BF16)
    return _encoder(x_prompt, p), _encoder(x_sample, p)
```
